```python
import math
import jax, jax.numpy as jnp
from jax import lax
import numpy as np

D_MODEL = 2048
BATCH = 8
SEQ = 2048
DEPTH = 1
DEC_BATCH = 4
DEC_SEQ = 2048
PAST_LEN = 128

GRID_W = 64
NA_HEADS = 8
NA_HEAD_DIM = 128
NA_WIN_H = 8
NA_WIN_W = 16
MLA_HEADS = 8
Q_LORA = 512
KV_LORA = 512
NOPE_DIM = 128
ROPE_DIM = 64
V_DIM = 128
ROPE_BASE = 10000.0
Q_BLOCK = 128
N_EXPERTS = 16
EC_CAPACITY = 2
EXPERT_FF = 1024
EPS = 1e-6

NA_WIDTH = NA_HEADS * NA_HEAD_DIM
MLA_WIDTH = MLA_HEADS * V_DIM
MIX_WIDTH = NA_WIDTH + MLA_WIDTH
QK_DIM = NOPE_DIM + ROPE_DIM
P_TOTAL = 3 * NA_WIDTH + Q_LORA + KV_LORA + ROPE_DIM
SPLITS = (NA_WIDTH, 2 * NA_WIDTH, 3 * NA_WIDTH, 3 * NA_WIDTH + Q_LORA, 3 * NA_WIDTH + Q_LORA + KV_LORA)

kernel_name = 'hymba_na_mla_expert_choice_encoder'


def rmsnorm(x, g):
    xf = x.astype(jnp.float32)
    y = xf * lax.rsqrt(jnp.mean(xf * xf, axis=-1, keepdims=True) + EPS)
    return (y * g.astype(jnp.float32)).astype(x.dtype)


def rope_tables(S):
    t = jnp.arange(S)
    row = (t // GRID_W).astype(jnp.float32)
    col = (t % GRID_W).astype(jnp.float32)
    nf = ROPE_DIM // 4
    inv = 1.0 / (ROPE_BASE ** (jnp.arange(nf, dtype=jnp.float32) / nf))
    ar = row[:, None] * inv
    ac = col[:, None] * inv
    return (jnp.cos(ar), jnp.sin(ar), jnp.cos(ac), jnp.sin(ac))


def rotate_half(x, cos, sin):
    x1, x2 = jnp.split(x, 2, axis=-1)
    return jnp.concatenate([x1 * cos - x2 * sin, x2 * cos + x1 * sin], axis=-1)


def rope_2d(x, tabs):
    cr, sr, cc, sc = [t.astype(x.dtype) for t in tabs]
    if x.ndim == 4:
        cr, sr, cc, sc = cr[:, None], sr[:, None], cc[:, None], sc[:, None]
    a = ROPE_DIM // 2
    return jnp.concatenate([rotate_half(x[..., :a], cr, sr), rotate_half(x[..., a:], cc, sc)], axis=-1)


def neighbourhood_attention(q, k, v, rpb):
    B, S, H, D = q.shape
    rows = S // GRID_W
    kh = min(NA_WIN_H, rows)
    kw = NA_WIN_W
    qg = q.reshape(B, rows, GRID_W, H, D)
    kg = k.reshape(B, rows, GRID_W, H, D)
    vg = v.reshape(B, rows, GRID_W, H, D)
    cols = np.arange(GRID_W)
    cstart = np.clip(cols - kw // 2, 0, GRID_W - kw)
    cidx = jnp.asarray(cstart[:, None] + np.arange(kw))
    coff = jnp.asarray(cstart[:, None] + np.arange(kw) - cols[:, None] + (NA_WIN_W - 1))
    scale = D ** -0.5

    def row_block(r):
        rs = jnp.clip(r - kh // 2, 0, rows - kh)
        ks = lax.dynamic_slice_in_dim(kg, rs, kh, axis=1)
        vs = lax.dynamic_slice_in_dim(vg, rs, kh, axis=1)
        kwin = ks[:, :, cidx]
        vwin = vs[:, :, cidx]
        qr = lax.dynamic_index_in_dim(qg, r, axis=1, keepdims=False)
        s = jnp.einsum('bchd,bicjhd->bhcij', qr, kwin).astype(jnp.float32) * scale
        roff = rs + jnp.arange(kh) - r + (NA_WIN_H - 1)
        bias = rpb[:, roff[None, :, None], coff[:, None, :]]
        s = s + bias.astype(jnp.float32)[None]
        p = jax.nn.softmax(s.reshape(B, H, GRID_W, kh * kw), axis=-1)
        p = p.reshape(B, H, GRID_W, kh, kw).astype(v.dtype)
        return jnp.einsum('bhcij,bicjhd->bchd', p, vwin)

    o = lax.map(row_block, jnp.arange(rows))
    return o.transpose(1, 0, 2, 3, 4).reshape(B, S, H * D)


def dense_bidirectional_attention(q, k, v):
    B, S, H, _ = q.shape
    nq = S // Q_BLOCK
    scale = QK_DIM ** -0.5
    qb = q.reshape(B, nq, Q_BLOCK, H, QK_DIM).transpose(1, 0, 2, 3, 4)

    def block(qi):
        s = jnp.einsum('bqhd,bkhd->bhqk', qi, k).astype(jnp.float32) * scale
        p = jax.nn.softmax(s, axis=-1).astype(v.dtype)
        return jnp.einsum('bhqk,bkhd->bqhd', p, v)

    o = lax.map(block, qb)
    return o.transpose(1, 0, 2, 3, 4).reshape(B, S, H * V_DIM)


def expert_choice_ffn(h, w_router, w_gate, w_up, w_down):
    N, D = h.shape
    cap = EC_CAPACITY * N // N_EXPERTS
    aff = jax.nn.softmax((h @ w_router).astype(jnp.float32), axis=-1)
    gates, idx = lax.top_k(aff.T, cap)
    xe = h[idx]
    hid = jax.nn.silu(jnp.einsum('ecd,edf->ecf', xe, w_gate)) * jnp.einsum('ecd,edf->ecf', xe, w_up)
    ye = jnp.einsum('ecf,efd->ecd', hid, w_down) * gates[..., None].astype(h.dtype)
    return jnp.zeros_like(h).at[idx.reshape(-1)].add(ye.reshape(-1, D))


def hybrid_layer(x, norm1_g, w_in, na_q_norm, na_k_norm, na_rpb, q_a_norm, w_q_up, kv_a_norm, w_kv_up,
                 mla_qn_norm, mla_qr_norm, mla_kn_norm, mla_kr_norm, na_out_norm, mla_out_norm, w_out,
                 norm2_g, w_router, w_gate, w_up, w_down):
    B, S, D = x.shape
    tabs = rope_tables(S)
    xn = rmsnorm(x, norm1_g)
    proj = xn @ w_in
    q_na, k_na, v_na, q_lat, kv_lat, k_pe = jnp.split(proj, SPLITS, axis=-1)

    q_na = rmsnorm(q_na.reshape(B, S, NA_HEADS, NA_HEAD_DIM), na_q_norm)
    k_na = rmsnorm(k_na.reshape(B, S, NA_HEADS, NA_HEAD_DIM), na_k_norm)
    v_na = v_na.reshape(B, S, NA_HEADS, NA_HEAD_DIM)
    out_na = neighbourhood_attention(q_na, k_na, v_na, na_rpb)

    q = (rmsnorm(q_lat, q_a_norm) @ w_q_up).reshape(B, S, MLA_HEADS, QK_DIM)
    kv = (rmsnorm(kv_lat, kv_a_norm) @ w_kv_up).reshape(B, S, MLA_HEADS, NOPE_DIM + V_DIM)
    q_nope = rmsnorm(q[..., :NOPE_DIM], mla_qn_norm)
    q_pe = rope_2d(rmsnorm(q[..., NOPE_DIM:], mla_qr_norm), tabs)
    k_nope = rmsnorm(kv[..., :NOPE_DIM], mla_kn_norm)
    v_mla = kv[..., NOPE_DIM:]
    k_pe = rope_2d(rmsnorm(k_pe, mla_kr_norm), tabs)
    q_mla = jnp.concatenate([q_nope, q_pe], axis=-1)
    k_mla = jnp.concatenate([k_nope, jnp.broadcast_to(k_pe[:, :, None, :], (B, S, MLA_HEADS, ROPE_DIM))], axis=-1)
    out_mla = dense_bidirectional_attention(q_mla, k_mla, v_mla)

    mixed = jnp.concatenate([rmsnorm(out_na, na_out_norm), rmsnorm(out_mla, mla_out_norm)], axis=-1)
    h = x + mixed @ w_out

    hn = rmsnorm(h, norm2_g).reshape(B * S, D)
    y = expert_choice_ffn(hn, w_router, w_gate, w_up, w_down).reshape(B, S, D)
    return h + y


def setup_inputs(seed: int = 0) -> dict:
    key = jax.random.key(seed)
    ks = jax.random.split(key, 24)
    f32 = jnp.float32
    L = DEPTH

    def nrm(k, shape, fan_in):
        return jax.random.normal(k, shape, f32) * (fan_in ** -0.5)

    def gain(k, shape):
        return 1.0 + 0.05 * jax.random.normal(k, shape, f32)

    return {
        'x_prompt': jax.random.normal(ks[0], (BATCH, SEQ, D_MODEL), f32),
        'x_sample': jax.random.normal(ks[1], (DEC_BATCH, DEC_SEQ, D_MODEL), f32),
        'norm1_g': gain(ks[2], (L, D_MODEL)),
        'w_in': nrm(ks[3], (L, D_MODEL, P_TOTAL), D_MODEL),
        'na_q_norm': gain(ks[4], (L, NA_HEAD_DIM)),
        'na_k_norm': gain(ks[5], (L, NA_HEAD_DIM)),
        'na_rpb': 0.02 * jax.random.normal(ks[6], (L, NA_HEADS, 2 * NA_WIN_H - 1, 2 * NA_WIN_W - 1), f32),
        'q_a_norm': gain(ks[7], (L, Q_LORA)),
        'w_q_up': nrm(ks[8], (L, Q_LORA, MLA_HEADS * QK_DIM), Q_LORA),
        'kv_a_norm': gain(ks[9], (L, KV_LORA)),
        'w_kv_up': nrm(ks[10], (L, KV_LORA, MLA_HEADS * (NOPE_DIM + V_DIM)), KV_LORA),
        'mla_qn_norm': gain(ks[11], (L, NOPE_DIM)),
        'mla_qr_norm': gain(ks[12], (L, ROPE_DIM)),
        'mla_kn_norm': gain(ks[13], (L, NOPE_DIM)),
        'mla_kr_norm': gain(ks[14], (L, ROPE_DIM)),
        'na_out_norm': gain(ks[15], (L, NA_WIDTH)),
        'mla_out_norm': gain(ks[16], (L, MLA_WIDTH)),
        'w_out': nrm(ks[17], (L, MIX_WIDTH, D_MODEL), MIX_WIDTH),
        'norm2_g': gain(ks[18], (L, D_MODEL)),
        'w_router': nrm(ks[19], (L, D_MODEL, N_EXPERTS), D_MODEL),
        'w_gate': nrm(ks[20], (L, N_EXPERTS, D_MODEL, EXPERT_FF), D_MODEL),
        'w_up': nrm(ks[21], (L, N_EXPERTS, D_MODEL, EXPERT_FF), D_MODEL),
        'w_down': nrm(ks[22], (L, N_EXPERTS, EXPERT_FF, D_MODEL), EXPERT_FF),
    }


def reference(x_prompt, x_sample, norm1_g, w_in, na_q_norm, na_k_norm, na_rpb, q_a_norm, w_q_up, kv_a_norm,
              w_kv_up, mla_qn_norm, mla_qr_norm, mla_kn_norm, mla_kr_norm, na_out_norm, mla_out_norm, w_out,
              norm2_g, w_router, w_gate, w_up, w_down):
    hp = x_prompt
    hs = x_sample
    for l in range(DEPTH):
        lp = (norm1_g[l], w_in[l], na_q_norm[l], na_k_norm[l], na_rpb[l], q_a_norm[l], w_q_up[l], kv_a_norm[l],
              w_kv_up[l], mla_qn_norm[l], mla_qr_norm[l], mla_kn_norm[l], mla_kr_norm[l], na_out_norm[l],
              mla_out_norm[l], w_out[l], norm2_g[l], w_router[l], w_gate[l], w_up[l], w_down[l])
        hp = hybrid_layer(hp, *lp)
        hs = hybrid_layer(hs, *lp)
    return (hp, hs)
```

```python
import functools

import jax
import jax.numpy as jnp
import numpy as np
from jax import lax
from jax.experimental import pallas as pl
from jax.experimental.pallas import tpu as pltpu

D_MODEL = 2048
GRID_W = 64
NA_HEADS = 8
NA_HEAD_DIM = 128
NA_WIN_H = 8
NA_WIN_W = 16
MLA_HEADS = 8
Q_LORA = 512
KV_LORA = 512
NOPE_DIM = 128
ROPE_DIM = 64
V_DIM = 128
ROPE_BASE = 10000.0
N_EXPERTS = 16
EC_CAPACITY = 2
EXPERT_FF = 1024
EPS = 1e-6

NA_WIDTH = NA_HEADS * NA_HEAD_DIM
MLA_WIDTH = MLA_HEADS * V_DIM
QK_DIM = NOPE_DIM + ROPE_DIM
LANES = 128
ROPE_PAD = LANES
P_PAD = 3 * NA_WIDTH + Q_LORA + KV_LORA + ROPE_PAD
NEG = -1e30

BF16 = jnp.bfloat16
F32 = jnp.float32

VMEM_LIMIT = 56 * 1024 * 1024


def _cparams(sem):
    return pltpu.CompilerParams(dimension_semantics=sem, vmem_limit_bytes=VMEM_LIMIT)


def _resident(shape):
    nd = len(shape)
    return pl.BlockSpec(shape, lambda *_: (0,) * nd, pipeline_mode=pl.Buffered(1))


def _rms(a, width):
    return lax.rsqrt(jnp.sum(a * a, axis=-1, keepdims=True) * (1.0 / width) + EPS)


def _rope(a, cos, sin):
    lane = lax.broadcasted_iota(jnp.int32, a.shape, 1)
    up = pltpu.roll(a, LANES - 16, axis=1)
    dn = pltpu.roll(a, 16, axis=1)
    partner = jnp.where((lane % 32) < 16, up, dn)
    return a * cos + partner * sin


def _proj_body(x_ref, g1_ref, win_ref, wq_ref, wkv_ref, gq_ref, gk_ref, gqa_ref, gkva_ref,
               gkr_ref, gqn_ref, gqr_ref, gkn_ref, cos_ref, sin_ref,
               qna_ref, kna_ref, vna_ref, qn_ref, qpe_ref, kn_ref, vm_ref, kpe_ref):
    x = x_ref[...]
    xn = (x * _rms(x, D_MODEL) * g1_ref[...]).astype(BF16)
    cos = cos_ref[...]
    sin = sin_ref[...]

    def proj(lo, hi):
        return jnp.dot(xn, win_ref[:, lo:hi], preferred_element_type=F32)

    def heads_to(ref, a, g, width, base, rope=False):
        for h in range(a.shape[1] // LANES):
            ah = a[:, h * LANES:(h + 1) * LANES]
            y = ah * _rms(ah, width) * g
            if rope:
                y = _rope(y, cos, sin)
            ref[:, base + h * LANES: base + (h + 1) * LANES] = y.astype(BF16)

    half = NA_WIDTH // 2
    for c in range(2):
        heads_to(qna_ref, proj(c * half, (c + 1) * half), gq_ref[...], NA_HEAD_DIM, c * half)
    for c in range(2):
        heads_to(kna_ref, proj(NA_WIDTH + c * half, NA_WIDTH + (c + 1) * half), gk_ref[...],
                 NA_HEAD_DIM, c * half)
    for c in range(2):
        vna_ref[:, c * half:(c + 1) * half] = proj(2 * NA_WIDTH + c * half,
                                                   2 * NA_WIDTH + (c + 1) * half).astype(BF16)

    o = 3 * NA_WIDTH
    ql = proj(o, o + Q_LORA)
    qln = (ql * _rms(ql, Q_LORA) * gqa_ref[...]).astype(BF16)
    kvl = proj(o + Q_LORA, o + Q_LORA + KV_LORA)
    kvln = (kvl * _rms(kvl, KV_LORA) * gkva_ref[...]).astype(BF16)
    kp = proj(o + Q_LORA + KV_LORA, P_PAD)
    heads_to(kpe_ref, kp, gkr_ref[...], ROPE_DIM, 0, rope=True)

    nh = MLA_HEADS * LANES
    for c in range(2):
        a = jnp.dot(qln, wq_ref[:, c * half:(c + 1) * half], preferred_element_type=F32)
        heads_to(qn_ref, a, gqn_ref[...], NOPE_DIM, c * half)
    for c in range(2):
        a = jnp.dot(qln, wq_ref[:, nh + c * half: nh + (c + 1) * half], preferred_element_type=F32)
        heads_to(qpe_ref, a, gqr_ref[...], ROPE_DIM, c * half, rope=True)
    for c in range(2):
        a = jnp.dot(kvln, wkv_ref[:, c * half:(c + 1) * half], preferred_element_type=F32)
        heads_to(kn_ref, a, gkn_ref[...], NOPE_DIM, c * half)
    for c in range(2):
        vm_ref[:, c * half:(c + 1) * half] = jnp.dot(
            kvln, wkv_ref[:, nh + c * half: nh + (c + 1) * half],
            preferred_element_type=F32).astype(BF16)


def _proj_call(x, p, tm):
    B, S, D = x.shape
    grid = (B, S // tm)
    row = lambda w: pl.BlockSpec((None, tm, w), lambda b, i: (b, i, 0))
    tab = pl.BlockSpec((tm, LANES), lambda b, i: (i, 0))
    vec = lambda w: _resident((1, w))
    out_w = (NA_WIDTH, NA_WIDTH, NA_WIDTH, MLA_HEADS * LANES, MLA_HEADS * LANES,
             MLA_HEADS * LANES, MLA_WIDTH, LANES)
    return pl.pallas_call(
        _proj_body,
        grid=grid,
        in_specs=[row(D), vec(D), _resident((D, P_PAD)), _resident(p['wq'].shape),
                  _resident(p['wkv'].shape), vec(LANES), vec(LANES), vec(Q_LORA), vec(KV_LORA),
                  vec(LANES), vec(LANES), vec(LANES), vec(LANES), tab, tab],
        out_specs=[row(w) for w in out_w],
        out_shape=[jax.ShapeDtypeStruct((B, S, w), BF16) for w in out_w],
        compiler_params=_cparams(("parallel", "arbitrary")),
        name="in_proj",
    )(x, p['g1'], p['win'], p['wq'], p['wkv'], p['gq'], p['gk'], p['gqa'], p['gkva'],
      p['gkr'], p['gqn'], p['gqr'], p['gkn'], p['cos'], p['sin'])


NA_QROWS = 8
NA_KROWS = 16


def _na_window_start(i, rows):
    return int(np.clip(NA_QROWS * i - NA_WIN_H // 2, 0, rows - NA_KROWS))


def _na_body(q_ref, k_ref, v_ref, bias_ref, o_ref, *, rows):
    tq = NA_QROWS * GRID_W
    tk = NA_KROWS * GRID_W
    for i in range(rows // NA_QROWS):
        ws = _na_window_start(i, rows) * GRID_W
        q = q_ref[i * tq:(i + 1) * tq, :]
        k = k_ref[ws:ws + tk, :]
        v = v_ref[ws:ws + tk, :]
        s = lax.dot_general(q, k, (((1,), (1,)), ((), ())), preferred_element_type=F32)
        s = s + bias_ref[i]
        m = jnp.max(s, axis=-1, keepdims=True)
        e = jnp.exp(s - m)
        l = jnp.sum(e, axis=-1, keepdims=True)
        o = jnp.dot(e.astype(BF16), v, preferred_element_type=F32)
        o_ref[i * tq:(i + 1) * tq, :] = (o / l).astype(BF16)


def _na_bias(rpb, rows):
    kh = min(NA_WIN_H, rows)
    nblk = rows // NA_QROWS
    qq = np.arange(NA_QROWS * GRID_W)
    kk = np.arange(NA_KROWS * GRID_W)
    ri = np.zeros((nblk, qq.size, kk.size), np.int32)
    ci = np.zeros_like(ri)
    valid = np.zeros(ri.shape, bool)
    for i in range(nblk):
        r = NA_QROWS * i + qq // GRID_W
        c = qq % GRID_W
        ki = _na_window_start(i, rows) + kk // GRID_W
        kj = kk % GRID_W
        rs = np.clip(r - kh // 2, 0, rows - kh)
        cs = np.clip(c - NA_WIN_W // 2, 0, GRID_W - NA_WIN_W)
        vr = (ki[None, :] >= rs[:, None]) & (ki[None, :] < rs[:, None] + kh)
        vc = (kj[None, :] >= cs[:, None]) & (kj[None, :] < cs[:, None] + NA_WIN_W)
        valid[i] = vr & vc
        ri[i] = np.clip(ki[None, :] - r[:, None] + NA_WIN_H - 1, 0, 2 * NA_WIN_H - 2)
        ci[i] = np.clip(kj[None, :] - c[:, None] + NA_WIN_W - 1, 0, 2 * NA_WIN_W - 2)
    flat = jnp.asarray(ri * (2 * NA_WIN_W - 1) + ci)
    tab = rpb.reshape(NA_HEADS, -1)
    b = jnp.take(tab, flat, axis=1)
    return jnp.where(jnp.asarray(valid)[None], b, NEG)


def _na_call(q, k, v, bias):
    B, S, _ = q.shape
    rows = S // GRID_W
    hb = lambda: pl.BlockSpec((None, S, LANES), lambda h, b: (b, 0, h))
    return pl.pallas_call(
        functools.partial(_na_body, rows=rows),
        grid=(NA_HEADS, B),
        in_specs=[hb(), hb(), hb(),
                  pl.BlockSpec((None,) + bias.shape[1:], lambda h, b: (h, 0, 0, 0))],
        out_specs=hb(),
        out_shape=jax.ShapeDtypeStruct((B, S, NA_WIDTH), BF16),
        compiler_params=_cparams(("arbitrary", "arbitrary")),
        name="na_attn",
    )(q, k, v, bias)


def _mla_body(qn_ref, qpe_ref, kn_ref, kpe_ref, v_ref, o_ref, kcat_ref):
    @pl.when(pl.program_id(2) == 0)
    def _():
        kcat_ref[:, :LANES] = kn_ref[...]
        kcat_ref[:, LANES:] = kpe_ref[...]

    q = jnp.concatenate([qn_ref[...], qpe_ref[...]], axis=1)
    s = lax.dot_general(q, kcat_ref[...], (((1,), (1,)), ((), ())), preferred_element_type=F32)
    m = jnp.max(s, axis=-1, keepdims=True)
    e = jnp.exp(s - m)
    l = jnp.sum(e, axis=-1, keepdims=True)
    o = jnp.dot(e.astype(BF16), v_ref[...], preferred_element_type=F32)
    o_ref[...] = (o / l).astype(BF16)


def _mla_call(qn, qpe, kn, kpe, v, tq):
    B, S, _ = qn.shape
    qb = lambda: pl.BlockSpec((None, tq, LANES), lambda b, h, i: (b, i, h))
    kb = lambda: pl.BlockSpec((None, S, LANES), lambda b, h, i: (b, 0, h))
    return pl.pallas_call(
        _mla_body,
        grid=(B, MLA_HEADS, S // tq),
        in_specs=[qb(), qb(), kb(), pl.BlockSpec((None, S, LANES), lambda b, h, i: (b, 0, 0)), kb()],
        out_specs=qb(),
        out_shape=jax.ShapeDtypeStruct((B, S, MLA_WIDTH), BF16),
        scratch_shapes=[pltpu.VMEM((S, 2 * LANES), BF16)],
        compiler_params=_cparams(("parallel", "arbitrary", "arbitrary")),
        name="mla_attn",
    )(qn, qpe, kn, kpe, v)


def _out_body(na_ref, mla_ref, x_ref, gna_ref, gmla_ref, wout_ref, g2_ref, wr_ref,
              h_ref, hn_ref, aff_ref):
    a = na_ref[...].astype(F32)
    an = (a * _rms(a, NA_WIDTH) * gna_ref[...]).astype(BF16)
    b = mla_ref[...].astype(F32)
    bn = (b * _rms(b, MLA_WIDTH) * gmla_ref[...]).astype(BF16)
    h = x_ref[...]
    h = h + jnp.dot(an, wout_ref[:NA_WIDTH, :], preferred_element_type=F32)
    h = h + jnp.dot(bn, wout_ref[NA_WIDTH:, :], preferred_element_type=F32)
    h_ref[...] = h
    hn = (h * _rms(h, D_MODEL) * g2_ref[...]).astype(BF16)
    hn_ref[...] = hn
    logits = jnp.dot(hn, wr_ref[...], preferred_element_type=F32)
    lane = lax.broadcasted_iota(jnp.int32, logits.shape, 1)
    logits = jnp.where(lane < N_EXPERTS, logits, NEG)
    m = jnp.max(logits, axis=-1, keepdims=True)
    e = jnp.exp(logits - m)
    aff = e / jnp.sum(e, axis=-1, keepdims=True)
    aff_ref[...] = aff.T[:N_EXPERTS, :]


def _out_call(na, mla, x, p, tm):
    B, S, D = x.shape
    row = lambda w: pl.BlockSpec((None, tm, w), lambda b, i: (b, i, 0))
    nblk = S // tm
    return pl.pallas_call(
        _out_body,
        grid=(B, nblk),
        in_specs=[row(NA_WIDTH), row(MLA_WIDTH), row(D), _resident((1, NA_WIDTH)),
                  _resident((1, MLA_WIDTH)), _resident((NA_WIDTH + MLA_WIDTH, D)),
                  _resident((1, D)), _resident((D, LANES))],
        out_specs=[row(D), row(D), pl.BlockSpec((N_EXPERTS, tm), lambda b, i: (0, b * nblk + i))],
        out_shape=[jax.ShapeDtypeStruct((B, S, D), F32), jax.ShapeDtypeStruct((B, S, D), BF16),
                   jax.ShapeDtypeStruct((N_EXPERTS, B * S), F32)],
        compiler_params=_cparams(("parallel", "arbitrary")),
        name="out_proj",
    )(na, mla, x, p['gna'], p['gmla'], p['wout'], p['g2'], p['wr'])


def _rope_tables(S):
    t = np.arange(S)
    row = (t // GRID_W).astype(np.float32)
    col = (t % GRID_W).astype(np.float32)
    nf = ROPE_DIM // 4
    inv = (1.0 / (ROPE_BASE ** (jnp.arange(nf, dtype=F32) / nf)))
    ar = jnp.asarray(row)[:, None] * inv
    ac = jnp.asarray(col)[:, None] * inv
    cr, sr, cc, sc = jnp.cos(ar), jnp.sin(ar), jnp.cos(ac), jnp.sin(ac)
    z = jnp.zeros((S, LANES - ROPE_DIM), F32)
    cos = jnp.concatenate([cr, cr, cc, cc, z], axis=1)
    sin = jnp.concatenate([-sr, sr, -sc, sc, z], axis=1)
    return cos, sin


def _pad_lanes(v, fill=0.0):
    return jnp.concatenate([v, jnp.full((LANES - v.shape[0],), fill, v.dtype)])[None, :]


def _prepare(S, norm1_g, w_in, na_q_norm, na_k_norm, na_rpb, q_a_norm, w_q_up, kv_a_norm, w_kv_up,
             mla_qn_norm, mla_qr_norm, mla_kn_norm, mla_kr_norm, na_out_norm, mla_out_norm, w_out,
             norm2_g, w_router):
    p = {}
    p['g1'] = norm1_g[None, :]
    p['win'] = jnp.concatenate(
        [w_in, jnp.zeros((D_MODEL, ROPE_PAD - ROPE_DIM), w_in.dtype)], axis=1).astype(BF16)
    wq = w_q_up.reshape(Q_LORA, MLA_HEADS, QK_DIM)
    wq_nope = wq[:, :, :NOPE_DIM].reshape(Q_LORA, MLA_HEADS * NOPE_DIM)
    wq_pe = jnp.pad(wq[:, :, NOPE_DIM:], ((0, 0), (0, 0), (0, LANES - ROPE_DIM)))
    p['wq'] = jnp.concatenate([wq_nope, wq_pe.reshape(Q_LORA, MLA_HEADS * LANES)], axis=1).astype(BF16)
    wkv = w_kv_up.reshape(KV_LORA, MLA_HEADS, NOPE_DIM + V_DIM)
    p['wkv'] = jnp.concatenate([wkv[:, :, :NOPE_DIM].reshape(KV_LORA, -1),
                                wkv[:, :, NOPE_DIM:].reshape(KV_LORA, -1)], axis=1).astype(BF16)
    p['gq'] = (na_q_norm * (NA_HEAD_DIM ** -0.5))[None, :]
    p['gk'] = na_k_norm[None, :]
    p['gqa'] = q_a_norm[None, :]
    p['gkva'] = kv_a_norm[None, :]
    p['gkr'] = _pad_lanes(mla_kr_norm)
    p['gqn'] = (mla_qn_norm * (QK_DIM ** -0.5))[None, :]
    p['gqr'] = _pad_lanes(mla_qr_norm * (QK_DIM ** -0.5))
    p['gkn'] = mla_kn_norm[None, :]
    p['cos'], p['sin'] = _rope_tables(S)
    p['bias'] = _na_bias(na_rpb, S // GRID_W)
    p['gna'] = na_out_norm[None, :]
    p['gmla'] = mla_out_norm[None, :]
    p['wout'] = w_out.astype(BF16)
    p['g2'] = norm2_g[None, :]
    p['wr'] = jnp.pad(w_router, ((0, 0), (0, LANES - N_EXPERTS))).astype(BF16)
    return p


def _attention_block(x, p):
    qna, kna, vna, qn, qpe, kn, vm, kpe = _proj_call(x, p, tm=512)
    out_na = _na_call(qna, kna, vna, p['bias'])
    out_mla = _mla_call(qn, qpe, kn, kpe, vm, tq=512)
    return _out_call(out_na, out_mla, x, p, tm=512)


def _moe_jnp(hn, aff_t, w_gate, w_up, w_down):
    N, D = hn.shape
    cap = EC_CAPACITY * N // N_EXPERTS
    gates, idx = lax.top_k(aff_t, cap)
    xe = hn[idx]
    hid = jax.nn.silu(jnp.einsum('ecd,edf->ecf', xe, w_gate.astype(BF16), preferred_element_type=F32)) * \
        jnp.einsum('ecd,edf->ecf', xe, w_up.astype(BF16), preferred_element_type=F32)
    ye = jnp.einsum('ecf,efd->ecd', hid.astype(BF16), w_down.astype(BF16), preferred_element_type=F32)
    ye = ye * gates[..., None]
    return jnp.zeros((N, D), F32).at[idx.reshape(-1)].add(ye.reshape(-1, D))


def _layer(x, p, w_gate, w_up, w_down):
    B, S, D = x.shape
    h, hn, aff_t = _attention_block(x, p)
    y = _moe_jnp(hn.reshape(B * S, D), aff_t, w_gate, w_up, w_down)
    return h + y.reshape(B, S, D)


def kernel(x_prompt, x_sample, norm1_g, w_in, na_q_norm, na_k_norm, na_rpb, q_a_norm, w_q_up, kv_a_norm, w_kv_up, mla_qn_norm, mla_qr_norm, mla_kn_norm, mla_kr_norm, na_out_norm, mla_out_norm, w_out, norm2_g, w_router, w_gate, w_up, w_down):
    hp, hs = x_prompt, x_sample
    for l in range(norm1_g.shape[0]):
        p = _prepare(hp.shape[1], norm1_g[l], w_in[l], na_q_norm[l], na_k_norm[l], na_rpb[l],
                     q_a_norm[l], w_q_up[l], kv_a_norm[l], w_kv_up[l], mla_qn_norm[l],
                     mla_qr_norm[l], mla_kn_norm[l], mla_kr_norm[l], na_out_norm[l],
                     mla_out_norm[l], w_out[l], norm2_g[l], w_router[l])
        hp = _layer(hp, p, w_gate[l], w_up[l], w_down[l])
        hs = _layer(hs, p, w_gate[l], w_up[l], w_down[l])
    return (hp, hs)
```

```python
import functools

import jax
import jax.numpy as jnp
import numpy as np
from jax import lax
from jax.experimental import pallas as pl
from jax.experimental.pallas import tpu as pltpu

D_MODEL = 2048
GRID_W = 64
NA_HEADS = 8
NA_HEAD_DIM = 128
NA_WIN_H = 8
NA_WIN_W = 16
MLA_HEADS = 8
Q_LORA = 512
KV_LORA = 512
NOPE_DIM = 128
ROPE_DIM = 64
V_DIM = 128
ROPE_BASE = 10000.0
N_EXPERTS = 16
EC_CAPACITY = 2
EXPERT_FF = 1024
EPS = 1e-6

NA_WIDTH = NA_HEADS * NA_HEAD_DIM
MLA_WIDTH = MLA_HEADS * V_DIM
QK_DIM = NOPE_DIM + ROPE_DIM
LANES = 128
ROPE_PAD = LANES
P_PAD = 3 * NA_WIDTH + Q_LORA + KV_LORA + ROPE_PAD
NEG = -1e30

BF16 = jnp.bfloat16
F32 = jnp.float32

VMEM_LIMIT = 56 * 1024 * 1024


def _cparams(sem):
    return pltpu.CompilerParams(dimension_semantics=sem, vmem_limit_bytes=VMEM_LIMIT)


def _resident(shape):
    nd = len(shape)
    return pl.BlockSpec(shape, lambda *_: (0,) * nd, pipeline_mode=pl.Buffered(1))


def _rms(a, width):
    return lax.rsqrt(jnp.sum(a * a, axis=-1, keepdims=True) * (1.0 / width) + EPS)


def _rope(a, cos, sin):
    lane = lax.broadcasted_iota(jnp.int32, a.shape, 1)
    up = pltpu.roll(a, LANES - 16, axis=1)
    dn = pltpu.roll(a, 16, axis=1)
    partner = jnp.where((lane % 32) < 16, up, dn)
    return a * cos + partner * sin


def _proj_body(x_ref, g1_ref, win_ref, wq_ref, wkv_ref, gq_ref, gk_ref, gqa_ref, gkva_ref,
               gkr_ref, gqn_ref, gqr_ref, gkn_ref, cos_ref, sin_ref,
               qna_ref, kna_ref, vna_ref, qn_ref, qpe_ref, kn_ref, vm_ref, kpe_ref):
    x = x_ref[...]
    xn = (x * _rms(x, D_MODEL) * g1_ref[...]).astype(BF16)
    cos = cos_ref[...]
    sin = sin_ref[...]

    def proj(lo, hi):
        return jnp.dot(xn, win_ref[:, lo:hi], preferred_element_type=F32)

    def heads_to(ref, a, g, width, base, rope=False):
        for h in range(a.shape[1] // LANES):
            ah = a[:, h * LANES:(h + 1) * LANES]
            y = ah * _rms(ah, width) * g
            if rope:
                y = _rope(y, cos, sin)
            ref[:, base + h * LANES: base + (h + 1) * LANES] = y.astype(BF16)

    half = NA_WIDTH // 2
    for c in range(2):
        heads_to(qna_ref, proj(c * half, (c + 1) * half), gq_ref[...], NA_HEAD_DIM, c * half)
    for c in range(2):
        heads_to(kna_ref, proj(NA_WIDTH + c * half, NA_WIDTH + (c + 1) * half), gk_ref[...],
                 NA_HEAD_DIM, c * half)
    for c in range(2):
        vna_ref[:, c * half:(c + 1) * half] = proj(2 * NA_WIDTH + c * half,
                                                   2 * NA_WIDTH + (c + 1) * half).astype(BF16)

    o = 3 * NA_WIDTH
    ql = proj(o, o + Q_LORA)
    qln = (ql * _rms(ql, Q_LORA) * gqa_ref[...]).astype(BF16)
    kvl = proj(o + Q_LORA, o + Q_LORA + KV_LORA)
    kvln = (kvl * _rms(kvl, KV_LORA) * gkva_ref[...]).astype(BF16)
    kp = proj(o + Q_LORA + KV_LORA, P_PAD)
    heads_to(kpe_ref, kp, gkr_ref[...], ROPE_DIM, 0, rope=True)

    nh = MLA_HEADS * LANES
    for c in range(2):
        a = jnp.dot(qln, wq_ref[:, c * half:(c + 1) * half], preferred_element_type=F32)
        heads_to(qn_ref, a, gqn_ref[...], NOPE_DIM, c * half)
    for c in range(2):
        a = jnp.dot(qln, wq_ref[:, nh + c * half: nh + (c + 1) * half], preferred_element_type=F32)
        heads_to(qpe_ref, a, gqr_ref[...], ROPE_DIM, c * half, rope=True)
    for c in range(2):
        a = jnp.dot(kvln, wkv_ref[:, c * half:(c + 1) * half], preferred_element_type=F32)
        heads_to(kn_ref, a, gkn_ref[...], NOPE_DIM, c * half)
    for c in range(2):
        vm_ref[:, c * half:(c + 1) * half] = jnp.dot(
            kvln, wkv_ref[:, nh + c * half: nh + (c + 1) * half],
            preferred_element_type=F32).astype(BF16)


def _proj_call(x, p, tm):
    B, S, D = x.shape
    grid = (B, S // tm)
    row = lambda w: pl.BlockSpec((None, tm, w), lambda b, i: (b, i, 0))
    tab = pl.BlockSpec((tm, LANES), lambda b, i: (i, 0))
    vec = lambda w: _resident((1, w))
    out_w = (NA_WIDTH, NA_WIDTH, NA_WIDTH, MLA_HEADS * LANES, MLA_HEADS * LANES,
             MLA_HEADS * LANES, MLA_WIDTH, LANES)
    return pl.pallas_call(
        _proj_body,
        grid=grid,
        in_specs=[row(D), vec(D), _resident((D, P_PAD)), _resident(p['wq'].shape),
                  _resident(p['wkv'].shape), vec(LANES), vec(LANES), vec(Q_LORA), vec(KV_LORA),
                  vec(LANES), vec(LANES), vec(LANES), vec(LANES), tab, tab],
        out_specs=[row(w) for w in out_w],
        out_shape=[jax.ShapeDtypeStruct((B, S, w), BF16) for w in out_w],
        compiler_params=_cparams(("parallel", "arbitrary")),
        name="in_proj",
    )(x, p['g1'], p['win'], p['wq'], p['wkv'], p['gq'], p['gk'], p['gqa'], p['gkva'],
      p['gkr'], p['gqn'], p['gqr'], p['gkn'], p['cos'], p['sin'])


NA_QROWS = 8
NA_KROWS = 16


def _na_window_start(i, rows):
    return int(np.clip(NA_QROWS * i - NA_WIN_H // 2, 0, rows - NA_KROWS))


def _na_build_bias(tab_ref, bias_ref, rows):
    kh = min(NA_WIN_H, rows)
    shape = (GRID_W, LANES)
    c = lax.broadcasted_iota(jnp.int32, shape, 0)
    l = lax.broadcasted_iota(jnp.int32, shape, 1)
    cs = jnp.clip(c - NA_WIN_W // 2, 0, GRID_W - NA_WIN_W)
    in_lo = (l >= cs) & (l < cs + NA_WIN_W)
    in_hi = (l - GRID_W >= cs) & (l - GRID_W < cs + NA_WIN_W)
    neg = jnp.full(shape, NEG, F32)
    lo, hi = [], []
    for d in range(2 * NA_WIN_H - 1):
        base = jnp.broadcast_to(tab_ref[d:d + 1, :], shape)
        lo.append(jnp.where(in_lo, pltpu.roll(base, 0, 1, stride=1, stride_axis=0), NEG))
        hi.append(jnp.where(in_hi, pltpu.roll(base, GRID_W, 1, stride=1, stride_axis=0), NEG))
    for i in range(rows // NA_QROWS):
        ws = _na_window_start(i, rows)
        for qr in range(NA_QROWS):
            r = NA_QROWS * i + qr
            rs = int(np.clip(r - kh // 2, 0, rows - kh))
            for m in range(NA_KROWS // 2):
                k0 = ws + 2 * m
                a = lo[k0 - r + NA_WIN_H - 1] if rs <= k0 < rs + kh else neg
                b = hi[k0 + 1 - r + NA_WIN_H - 1] if rs <= k0 + 1 < rs + kh else neg
                bias_ref[i, qr * GRID_W:(qr + 1) * GRID_W, m * LANES:(m + 1) * LANES] = (
                    jnp.where(l < GRID_W, a, b))


def _na_body(q_ref, k_ref, v_ref, tab_ref, o_ref, bias_ref, *, rows):
    tq = NA_QROWS * GRID_W
    tk = NA_KROWS * GRID_W

    @pl.when(pl.program_id(1) == 0)
    def _():
        _na_build_bias(tab_ref, bias_ref, rows)

    for i in range(rows // NA_QROWS):
        ws = _na_window_start(i, rows) * GRID_W
        q = q_ref[i * tq:(i + 1) * tq, :]
        k = k_ref[ws:ws + tk, :]
        v = v_ref[ws:ws + tk, :]
        s = lax.dot_general(q, k, (((1,), (1,)), ((), ())), preferred_element_type=F32)
        s = s + bias_ref[i]
        m = jnp.max(s, axis=-1, keepdims=True)
        e = jnp.exp(s - m)
        l = jnp.sum(e, axis=-1, keepdims=True)
        o = jnp.dot(e.astype(BF16), v, preferred_element_type=F32)
        o_ref[i * tq:(i + 1) * tq, :] = (o / l).astype(BF16)


def _na_bias_table(rpb):
    H, nd, nc = rpb.shape
    half = nc // 2
    fill = jnp.full((H, nd, LANES - nc), NEG, rpb.dtype)
    tab = jnp.concatenate([rpb[:, :, half:], fill, rpb[:, :, :half]], axis=2)
    return jnp.concatenate([tab, jnp.full((H, 16 - nd, LANES), NEG, rpb.dtype)], axis=1)


def _na_call(q, k, v, tab):
    B, S, _ = q.shape
    rows = S // GRID_W
    hb = lambda: pl.BlockSpec((None, S, LANES), lambda h, b: (b, 0, h))
    return pl.pallas_call(
        functools.partial(_na_body, rows=rows),
        grid=(NA_HEADS, B),
        in_specs=[hb(), hb(), hb(),
                  pl.BlockSpec((None,) + tab.shape[1:], lambda h, b: (h, 0, 0))],
        out_specs=hb(),
        out_shape=jax.ShapeDtypeStruct((B, S, NA_WIDTH), BF16),
        scratch_shapes=[pltpu.VMEM((rows // NA_QROWS, NA_QROWS * GRID_W, NA_KROWS * GRID_W), F32)],
        compiler_params=_cparams(("arbitrary", "arbitrary")),
        name="na_attn",
    )(q, k, v, tab)


def _mla_body(qn_ref, qpe_ref, kn_ref, kpe_ref, v_ref, o_ref, kcat_ref):
    @pl.when(pl.program_id(2) == 0)
    def _():
        kcat_ref[:, :LANES] = kn_ref[...]
        kcat_ref[:, LANES:] = kpe_ref[...]

    q = jnp.concatenate([qn_ref[...], qpe_ref[...]], axis=1)
    s = lax.dot_general(q, kcat_ref[...], (((1,), (1,)), ((), ())), preferred_element_type=F32)
    m = jnp.max(s, axis=-1, keepdims=True)
    e = jnp.exp(s - m)
    l = jnp.sum(e, axis=-1, keepdims=True)
    o = jnp.dot(e.astype(BF16), v_ref[...], preferred_element_type=F32)
    o_ref[...] = (o / l).astype(BF16)


def _mla_call(qn, qpe, kn, kpe, v, tq):
    B, S, _ = qn.shape
    qb = lambda: pl.BlockSpec((None, tq, LANES), lambda b, h, i: (b, i, h))
    kb = lambda: pl.BlockSpec((None, S, LANES), lambda b, h, i: (b, 0, h))
    return pl.pallas_call(
        _mla_body,
        grid=(B, MLA_HEADS, S // tq),
        in_specs=[qb(), qb(), kb(), pl.BlockSpec((None, S, LANES), lambda b, h, i: (b, 0, 0)), kb()],
        out_specs=qb(),
        out_shape=jax.ShapeDtypeStruct((B, S, MLA_WIDTH), BF16),
        scratch_shapes=[pltpu.VMEM((S, 2 * LANES), BF16)],
        compiler_params=_cparams(("parallel", "arbitrary", "arbitrary")),
        name="mla_attn",
    )(qn, qpe, kn, kpe, v)


def _out_body(na_ref, mla_ref, x_ref, gna_ref, gmla_ref, wout_ref, g2_ref, wr_ref,
              h_ref, hn_ref, aff_ref):
    a = na_ref[...].astype(F32)
    an = (a * _rms(a, NA_WIDTH) * gna_ref[...]).astype(BF16)
    b = mla_ref[...].astype(F32)
    bn = (b * _rms(b, MLA_WIDTH) * gmla_ref[...]).astype(BF16)
    h = x_ref[...]
    h = h + jnp.dot(an, wout_ref[:NA_WIDTH, :], preferred_element_type=F32)
    h = h + jnp.dot(bn, wout_ref[NA_WIDTH:, :], preferred_element_type=F32)
    h_ref[...] = h
    hn = h * _rms(h, D_MODEL) * g2_ref[...]
    hn_ref[...] = hn
    logits = jnp.dot(hn.astype(BF16), wr_ref[...], preferred_element_type=F32)
    lane = lax.broadcasted_iota(jnp.int32, logits.shape, 1)
    logits = jnp.where(lane < N_EXPERTS, logits, NEG)
    m = jnp.max(logits, axis=-1, keepdims=True)
    e = jnp.exp(logits - m)
    aff = e / jnp.sum(e, axis=-1, keepdims=True)
    aff_ref[...] = aff.T[:N_EXPERTS, :]


def _out_call(na, mla, x, p, tm):
    B, S, D = x.shape
    row = lambda w: pl.BlockSpec((None, tm, w), lambda b, i: (b, i, 0))
    nblk = S // tm
    return pl.pallas_call(
        _out_body,
        grid=(B, nblk),
        in_specs=[row(NA_WIDTH), row(MLA_WIDTH), row(D), _resident((1, NA_WIDTH)),
                  _resident((1, MLA_WIDTH)), _resident((NA_WIDTH + MLA_WIDTH, D)),
                  _resident((1, D)), _resident((D, LANES))],
        out_specs=[row(D), row(D), pl.BlockSpec((N_EXPERTS, tm), lambda b, i: (0, b * nblk + i))],
        out_shape=[jax.ShapeDtypeStruct((B, S, D), F32), jax.ShapeDtypeStruct((B, S, D), F32),
                   jax.ShapeDtypeStruct((N_EXPERTS, B * S), F32)],
        compiler_params=_cparams(("parallel", "arbitrary")),
        name="out_proj",
    )(na, mla, x, p['gna'], p['gmla'], p['wout'], p['g2'], p['wr'])


def _tri(n, strict, upper):
    r = lax.broadcasted_iota(jnp.int32, (n, n), 0)
    c = lax.broadcasted_iota(jnp.int32, (n, n), 1)
    if upper:
        m = (r < c) if strict else (r <= c)
    else:
        m = (c < r) if strict else (c <= r)
    return jnp.where(m, 1.0, 0.0).astype(BF16)


def _dot_t(a, b):
    return lax.dot_general(a, b, (((1,), (1,)), ((), ())), preferred_element_type=F32)


def _topk_body(aff_ref, idx_ref, gate_ref, sel_ref, cum_ref, *, nb, cap):
    E = N_EXPERTS
    aff = aff_ref[...]
    bits = pltpu.bitcast(aff, jnp.int32)

    def count(mask):
        c = jnp.sum(jnp.where(mask, 1.0, 0.0), axis=1, keepdims=True)
        return jnp.sum(c, axis=2, keepdims=True)

    def search(b, t):
        cand = t | jnp.left_shift(jnp.int32(1), 30 - b)
        return jnp.where(count(bits >= cand) >= cap, cand, t)

    t = lax.fori_loop(0, 31, search, jnp.zeros((E, 1, 1), jnp.int32))

    upper_incl = _tri(LANES, strict=False, upper=True)
    lower_strict = jnp.broadcast_to(_tri(nb, strict=True, upper=False)[None], (E, nb, nb))

    def cumsum_incl(m):
        mb = m.astype(BF16)
        within = jnp.dot(mb.reshape(E * nb, LANES), upper_incl,
                         preferred_element_type=F32).reshape(E, nb, LANES)
        before = lax.dot_general(lower_strict, mb, (((2,), (1,)), ((0,), (0,))),
                                 preferred_element_type=F32)
        return within + jnp.sum(before, axis=2, keepdims=True)

    gt = bits > t
    eq = jnp.where(bits == t, 1.0, 0.0)
    need = cap - count(gt)
    eq_before = cumsum_incl(eq) - eq
    sel = jnp.where(gt | ((eq > 0.5) & (eq_before < need)), 1.0, 0.0)
    sel_ref[...] = sel
    cum_ref[...] = cumsum_incl(sel)

    ones_rows = jnp.ones((8, LANES), BF16)
    lane0 = jnp.where(lax.broadcasted_iota(jnp.int32, (8, LANES), 1) == 0, 1.0, 0.0).astype(BF16)
    ones_nb = jnp.ones((nb, LANES), BF16)
    ones_sq = jnp.ones((LANES, LANES), BF16)
    upper_strict_nb = _tri(nb, strict=True, upper=True)
    slot_nb = lax.broadcasted_iota(jnp.int32, (cap, nb), 0).astype(F32)
    slot = lax.broadcasted_iota(jnp.int32, (cap, LANES), 0).astype(F32)
    lane = lax.broadcasted_iota(jnp.int32, (cap, LANES), 1).astype(F32)

    def compact(e, carry):
        m = sel_ref[e].astype(BF16)
        c = cum_ref[e]
        row_tot = _dot_t(ones_rows, m)
        row_off = jnp.dot(row_tot.astype(BF16), upper_strict_nb, preferred_element_type=F32)
        row_end = row_off + row_tot
        off_b = row_off[0:1, :]
        end_b = row_end[0:1, :]
        onehot = jnp.where((off_b <= slot_nb) & (slot_nb < end_b), 1.0, 0.0).astype(BF16)
        rows_before = jnp.where(end_b <= slot_nb, 1.0, 0.0).astype(BF16)
        row_idx = jnp.dot(rows_before, ones_nb, preferred_element_type=F32)
        c_hi = jnp.floor(c * (1.0 / LANES))
        c_lo = c - c_hi * LANES
        c_row = (jnp.dot(onehot, c_hi.astype(BF16), preferred_element_type=F32) * LANES
                 + jnp.dot(onehot, c_lo.astype(BF16), preferred_element_type=F32))
        before = jnp.where(c_row <= slot, 1.0, 0.0).astype(BF16)
        lane_idx = jnp.dot(before, ones_sq, preferred_element_type=F32)
        tok = (_dot_t(lane0, row_idx.astype(BF16)) * LANES
               + _dot_t(lane0, lane_idx.astype(BF16)))
        idx_ref[e] = tok.astype(jnp.int32)
        a = aff_ref[e]
        a1 = a.astype(BF16)
        r1 = a - a1.astype(F32)
        a2 = r1.astype(BF16)
        a3 = (r1 - a2.astype(F32)).astype(BF16)
        a_row = (jnp.dot(onehot, a1, preferred_element_type=F32)
                 + jnp.dot(onehot, a2, preferred_element_type=F32)
                 + jnp.dot(onehot, a3, preferred_element_type=F32))
        gate_ref[e] = jnp.sum(jnp.where(lane == lane_idx, a_row, 0.0), axis=1, keepdims=True)
        return carry

    lax.fori_loop(0, E, compact, 0)


def _topk_call(aff_t):
    E, N = aff_t.shape
    nb = N // LANES
    cap = EC_CAPACITY * N // N_EXPERTS
    idx, gates = pl.pallas_call(
        functools.partial(_topk_body, nb=nb, cap=cap),
        out_shape=[jax.ShapeDtypeStruct((E, 8, cap), jnp.int32),
                   jax.ShapeDtypeStruct((E, cap, 1), F32)],
        scratch_shapes=[pltpu.VMEM((E, nb, LANES), F32), pltpu.VMEM((E, nb, LANES), F32)],
        compiler_params=pltpu.CompilerParams(vmem_limit_bytes=VMEM_LIMIT),
        name="ec_topk",
    )(aff_t.reshape(E, nb, LANES))
    return idx[:, 0, :], gates


FFN_ROWS = 1024
FFN_FC = 256
FFN_DC = 512


def _ffn_body(idx_ref, hn_hbm, yin_hbm, gate_ref, wg_ref, wu_ref, wd_ref, y_hbm,
              rows_ref, xe_ref, acc_ref, sem, *, cap, rows):
    del yin_hbm
    e, r, f = pl.program_id(0), pl.program_id(1), pl.program_id(2)
    base = e * cap + r * rows

    def gather(src_hbm, s):
        return pltpu.make_async_copy(src_hbm.at[idx_ref[base + s]], rows_ref.at[s], sem.at[0])

    def scatter(s):
        return pltpu.make_async_copy(rows_ref.at[s], y_hbm.at[idx_ref[base + s]], sem.at[1])

    def for_rows(fn):
        def body(s, c):
            fn(s)
            return c
        lax.fori_loop(0, rows, body, 0, unroll=8)

    @pl.when(f == 0)
    def _():
        for_rows(lambda s: gather(hn_hbm, s).start())
        for_rows(lambda s: gather(hn_hbm, s).wait())
        xe_ref[...] = rows_ref[...].astype(BF16)

    x = xe_ref[...]
    g = jnp.dot(x, wg_ref[...].astype(BF16), preferred_element_type=F32)
    u = jnp.dot(x, wu_ref[...].astype(BF16), preferred_element_type=F32)
    hid = (jax.nn.silu(g) * u).astype(BF16)
    wd = wd_ref[...].astype(BF16)

    def down(accumulate):
        for c in range(D_MODEL // FFN_DC):
            cs = slice(c * FFN_DC, (c + 1) * FFN_DC)
            part = jnp.dot(hid, wd[:, cs], preferred_element_type=F32)
            acc_ref[:, cs] = acc_ref[:, cs] + part if accumulate else part

    pl.when(f == 0)(lambda: down(False))
    pl.when(f != 0)(lambda: down(True))

    @pl.when(f == pl.num_programs(2) - 1)
    def _():
        for_rows(lambda s: gather(y_hbm, s).start())
        for_rows(lambda s: gather(y_hbm, s).wait())
        rows_ref[...] = rows_ref[...] + acc_ref[...] * gate_ref[...]
        for_rows(lambda s: scatter(s).start())
        for_rows(lambda s: scatter(s).wait())


def _ffn_call(idx, gates, hn, h, w_gate, w_up, w_down):
    E, cap = idx.shape
    N, D = hn.shape
    F = w_gate.shape[2]
    rows = min(FFN_ROWS, cap)
    grid_spec = pltpu.PrefetchScalarGridSpec(
        num_scalar_prefetch=1,
        grid=(E, cap // rows, F // FFN_FC),
        in_specs=[
            pl.BlockSpec(memory_space=pl.ANY),
            pl.BlockSpec(memory_space=pl.ANY),
            pl.BlockSpec((None, rows, 1), lambda e, r, f, idx: (e, r, 0)),
            pl.BlockSpec((None, D, FFN_FC), lambda e, r, f, idx: (e, 0, f)),
            pl.BlockSpec((None, D, FFN_FC), lambda e, r, f, idx: (e, 0, f)),
            pl.BlockSpec((None, FFN_FC, D), lambda e, r, f, idx: (e, f, 0)),
        ],
        out_specs=pl.BlockSpec(memory_space=pl.ANY),
        scratch_shapes=[pltpu.VMEM((rows, D), F32), pltpu.VMEM((rows, D), BF16),
                        pltpu.VMEM((rows, D), F32), pltpu.SemaphoreType.DMA((2,))],
    )
    return pl.pallas_call(
        functools.partial(_ffn_body, cap=cap, rows=rows),
        grid_spec=grid_spec,
        out_shape=jax.ShapeDtypeStruct((N, D), F32),
        input_output_aliases={2: 0},
        compiler_params=_cparams(("arbitrary", "arbitrary", "arbitrary")),
        name="ec_ffn",
    )(idx.reshape(-1), hn, h, gates, w_gate, w_up, w_down)


def _rope_tables(S):
    t = np.arange(S)
    row = (t // GRID_W).astype(np.float32)
    col = (t % GRID_W).astype(np.float32)
    nf = ROPE_DIM // 4
    inv = (1.0 / (ROPE_BASE ** (jnp.arange(nf, dtype=F32) / nf)))
    ar = jnp.asarray(row)[:, None] * inv
    ac = jnp.asarray(col)[:, None] * inv
    cr, sr, cc, sc = jnp.cos(ar), jnp.sin(ar), jnp.cos(ac), jnp.sin(ac)
    z = jnp.zeros((S, LANES - ROPE_DIM), F32)
    cos = jnp.concatenate([cr, cr, cc, cc, z], axis=1)
    sin = jnp.concatenate([-sr, sr, -sc, sc, z], axis=1)
    return cos, sin


def _pad_lanes(v, fill=0.0):
    return jnp.concatenate([v, jnp.full((LANES - v.shape[0],), fill, v.dtype)])[None, :]


def _prepare(S, norm1_g, w_in, na_q_norm, na_k_norm, na_rpb, q_a_norm, w_q_up, kv_a_norm, w_kv_up,
             mla_qn_norm, mla_qr_norm, mla_kn_norm, mla_kr_norm, na_out_norm, mla_out_norm, w_out,
             norm2_g, w_router):
    p = {}
    p['g1'] = norm1_g[None, :]
    p['win'] = jnp.concatenate(
        [w_in, jnp.zeros((D_MODEL, ROPE_PAD - ROPE_DIM), w_in.dtype)], axis=1).astype(BF16)
    wq = w_q_up.reshape(Q_LORA, MLA_HEADS, QK_DIM)
    wq_nope = wq[:, :, :NOPE_DIM].reshape(Q_LORA, MLA_HEADS * NOPE_DIM)
    wq_pe = jnp.pad(wq[:, :, NOPE_DIM:], ((0, 0), (0, 0), (0, LANES - ROPE_DIM)))
    p['wq'] = jnp.concatenate([wq_nope, wq_pe.reshape(Q_LORA, MLA_HEADS * LANES)], axis=1).astype(BF16)
    wkv = w_kv_up.reshape(KV_LORA, MLA_HEADS, NOPE_DIM + V_DIM)
    p['wkv'] = jnp.concatenate([wkv[:, :, :NOPE_DIM].reshape(KV_LORA, -1),
                                wkv[:, :, NOPE_DIM:].reshape(KV_LORA, -1)], axis=1).astype(BF16)
    p['gq'] = (na_q_norm * (NA_HEAD_DIM ** -0.5))[None, :]
    p['gk'] = na_k_norm[None, :]
    p['gqa'] = q_a_norm[None, :]
    p['gkva'] = kv_a_norm[None, :]
    p['gkr'] = _pad_lanes(mla_kr_norm)
    p['gqn'] = (mla_qn_norm * (QK_DIM ** -0.5))[None, :]
    p['gqr'] = _pad_lanes(mla_qr_norm * (QK_DIM ** -0.5))
    p['gkn'] = mla_kn_norm[None, :]
    p['cos'], p['sin'] = _rope_tables(S)
    p['bias'] = _na_bias_table(na_rpb)
    p['gna'] = na_out_norm[None, :]
    p['gmla'] = mla_out_norm[None, :]
    p['wout'] = w_out.astype(BF16)
    p['g2'] = norm2_g[None, :]
    p['wr'] = jnp.pad(w_router, ((0, 0), (0, LANES - N_EXPERTS))).astype(BF16)
    return p


def _attention_block(x, p):
    qna, kna, vna, qn, qpe, kn, vm, kpe = _proj_call(x, p, tm=512)
    out_na = _na_call(qna, kna, vna, p['bias'])
    out_mla = _mla_call(qn, qpe, kn, kpe, vm, tq=512)
    return _out_call(out_na, out_mla, x, p, tm=512)


def _layer(x, p, w_gate, w_up, w_down):
    B, S, D = x.shape
    h, hn, aff_t = _attention_block(x, p)
    idx, gates = _topk_call(aff_t)
    y = _ffn_call(idx, gates, hn.reshape(B * S, D), h.reshape(B * S, D), w_gate, w_up, w_down)
    return y.reshape(B, S, D)


def kernel(x_prompt, x_sample, norm1_g, w_in, na_q_norm, na_k_norm, na_rpb, q_a_norm, w_q_up, kv_a_norm, w_kv_up, mla_qn_norm, mla_qr_norm, mla_kn_norm, mla_kr_norm, na_out_norm, mla_out_norm, w_out, norm2_g, w_router, w_gate, w_up, w_down):
    hp, hs = x_prompt, x_sample
    for l in range(norm1_g.shape[0]):
        p = _prepare(hp.shape[1], norm1_g[l], w_in[l], na_q_norm[l], na_k_norm[l], na_rpb[l],
                     q_a_norm[l], w_q_up[l], kv_a_norm[l], w_kv_up[l], mla_qn_norm[l],
                     mla_qr_norm[l], mla_kn_norm[l], mla_kr_norm[l], na_out_norm[l],
                     mla_out_norm[l], w_out[l], norm2_g[l], w_router[l])
        hp = _layer(hp, p, w_gate[l], w_up[l], w_down[l])
        hs = _layer(hs, p, w_gate[l], w_up[l], w_down[l])
    return (hp, hs)
```

```python
import functools

import jax
import jax.numpy as jnp
import numpy as np
from jax import lax
from jax.experimental import pallas as pl
from jax.experimental.pallas import tpu as pltpu

D_MODEL = 2048
GRID_W = 64
NA_HEADS = 8
NA_HEAD_DIM = 128
NA_WIN_H = 8
NA_WIN_W = 16
MLA_HEADS = 8
Q_LORA = 512
KV_LORA = 512
NOPE_DIM = 128
ROPE_DIM = 64
V_DIM = 128
ROPE_BASE = 10000.0
N_EXPERTS = 16
EC_CAPACITY = 2
EXPERT_FF = 1024
EPS = 1e-6

NA_WIDTH = NA_HEADS * NA_HEAD_DIM
MLA_WIDTH = MLA_HEADS * V_DIM
QK_DIM = NOPE_DIM + ROPE_DIM
LANES = 128
ROPE_PAD = LANES
P_PAD = 3 * NA_WIDTH + Q_LORA + KV_LORA + ROPE_PAD
NEG = -1e30
LOG2E = 1.4426950408889634

BF16 = jnp.bfloat16
F32 = jnp.float32

VMEM_LIMIT = 56 * 1024 * 1024


def _cparams(sem):
    return pltpu.CompilerParams(dimension_semantics=sem, vmem_limit_bytes=VMEM_LIMIT)


def _resident(shape):
    nd = len(shape)
    return pl.BlockSpec(shape, lambda *_: (0,) * nd, pipeline_mode=pl.Buffered(1))


def _rms(a, width):
    return lax.rsqrt(jnp.sum(a * a, axis=-1, keepdims=True) * (1.0 / width) + EPS)


def _rope(a, cos, sin):
    lane = lax.broadcasted_iota(jnp.int32, a.shape, 1)
    up = pltpu.roll(a, LANES - 16, axis=1)
    dn = pltpu.roll(a, 16, axis=1)
    partner = jnp.where((lane % 32) < 16, up, dn)
    return a * cos + partner * sin


def _proj_body(x_ref, g1_ref, win_ref, wq_ref, wkv_ref, gq_ref, gk_ref, gqa_ref, gkva_ref,
               gkr_ref, gqn_ref, gqr_ref, gkn_ref, cos_ref, sin_ref,
               qna_ref, kna_ref, vna_ref, qn_ref, qpe_ref, kn_ref, vm_ref, kpe_ref):
    x = x_ref[...]
    xn = (x * _rms(x, D_MODEL) * g1_ref[...]).astype(BF16)
    cos = cos_ref[...]
    sin = sin_ref[...]

    def proj(lo, hi):
        return jnp.dot(xn, win_ref[:, lo:hi], preferred_element_type=F32)

    def heads_to(ref, a, g, width, base, rope=False):
        for h in range(a.shape[1] // LANES):
            ah = a[:, h * LANES:(h + 1) * LANES]
            y = ah * _rms(ah, width) * g
            if rope:
                y = _rope(y, cos, sin)
            ref[:, base + h * LANES: base + (h + 1) * LANES] = y.astype(BF16)

    half = NA_WIDTH // 2
    for c in range(2):
        heads_to(qna_ref, proj(c * half, (c + 1) * half), gq_ref[...], NA_HEAD_DIM, c * half)
    for c in range(2):
        heads_to(kna_ref, proj(NA_WIDTH + c * half, NA_WIDTH + (c + 1) * half), gk_ref[...],
                 NA_HEAD_DIM, c * half)
    for c in range(2):
        vna_ref[:, c * half:(c + 1) * half] = proj(2 * NA_WIDTH + c * half,
                                                   2 * NA_WIDTH + (c + 1) * half).astype(BF16)

    o = 3 * NA_WIDTH
    ql = proj(o, o + Q_LORA)
    qln = (ql * _rms(ql, Q_LORA) * gqa_ref[...]).astype(BF16)
    kvl = proj(o + Q_LORA, o + Q_LORA + KV_LORA)
    kvln = (kvl * _rms(kvl, KV_LORA) * gkva_ref[...]).astype(BF16)
    kp = proj(o + Q_LORA + KV_LORA, P_PAD)
    heads_to(kpe_ref, kp, gkr_ref[...], ROPE_DIM, 0, rope=True)

    nh = MLA_HEADS * LANES
    for c in range(2):
        a = jnp.dot(qln, wq_ref[:, c * half:(c + 1) * half], preferred_element_type=F32)
        heads_to(qn_ref, a, gqn_ref[...], NOPE_DIM, c * half)
    for c in range(2):
        a = jnp.dot(qln, wq_ref[:, nh + c * half: nh + (c + 1) * half], preferred_element_type=F32)
        heads_to(qpe_ref, a, gqr_ref[...], ROPE_DIM, c * half, rope=True)
    for c in range(2):
        a = jnp.dot(kvln, wkv_ref[:, c * half:(c + 1) * half], preferred_element_type=F32)
        heads_to(kn_ref, a, gkn_ref[...], NOPE_DIM, c * half)
    for c in range(2):
        vm_ref[:, c * half:(c + 1) * half] = jnp.dot(
            kvln, wkv_ref[:, nh + c * half: nh + (c + 1) * half],
            preferred_element_type=F32).astype(BF16)


def _proj_call(x, p, tm):
    B, S, D = x.shape
    grid = (B, S // tm)
    row = lambda w: pl.BlockSpec((None, tm, w), lambda b, i: (b, i, 0))
    tab = pl.BlockSpec((tm, LANES), lambda b, i: (i, 0))
    vec = lambda w: _resident((1, w))
    out_w = (NA_WIDTH, NA_WIDTH, NA_WIDTH, MLA_HEADS * LANES, MLA_HEADS * LANES,
             MLA_HEADS * LANES, MLA_WIDTH, LANES)
    return pl.pallas_call(
        _proj_body,
        grid=grid,
        in_specs=[row(D), vec(D), _resident((D, P_PAD)), _resident(p['wq'].shape),
                  _resident(p['wkv'].shape), vec(LANES), vec(LANES), vec(Q_LORA), vec(KV_LORA),
                  vec(LANES), vec(LANES), vec(LANES), vec(LANES), tab, tab],
        out_specs=[row(w) for w in out_w],
        out_shape=[jax.ShapeDtypeStruct((B, S, w), BF16) for w in out_w],
        compiler_params=_cparams(("parallel", "arbitrary")),
        name="in_proj",
    )(x, p['g1'], p['win'], p['wq'], p['wkv'], p['gq'], p['gk'], p['gqa'], p['gkva'],
      p['gkr'], p['gqn'], p['gqr'], p['gkn'], p['cos'], p['sin'])


NA_QROWS = 8
NA_KROWS = 16


def _na_window_start(i, rows):
    return int(np.clip(NA_QROWS * i - NA_WIN_H // 2, 0, rows - NA_KROWS))


def _na_build_bias(tab_ref, bias_ref, rows):
    kh = min(NA_WIN_H, rows)
    shape = (GRID_W, LANES)
    c = lax.broadcasted_iota(jnp.int32, shape, 0)
    l = lax.broadcasted_iota(jnp.int32, shape, 1)
    cs = jnp.clip(c - NA_WIN_W // 2, 0, GRID_W - NA_WIN_W)
    in_lo = (l >= cs) & (l < cs + NA_WIN_W)
    in_hi = (l - GRID_W >= cs) & (l - GRID_W < cs + NA_WIN_W)
    neg = jnp.full(shape, NEG, F32)
    lo, hi = [], []
    for d in range(2 * NA_WIN_H - 1):
        base = jnp.broadcast_to(tab_ref[d:d + 1, :], shape)
        lo.append(jnp.where(in_lo, pltpu.roll(base, 0, 1, stride=1, stride_axis=0), NEG))
        hi.append(jnp.where(in_hi, pltpu.roll(base, GRID_W, 1, stride=1, stride_axis=0), NEG))
    for i in range(rows // NA_QROWS):
        ws = _na_window_start(i, rows)
        for qr in range(NA_QROWS):
            r = NA_QROWS * i + qr
            rs = int(np.clip(r - kh // 2, 0, rows - kh))
            for m in range(NA_KROWS // 2):
                k0 = ws + 2 * m
                a = lo[k0 - r + NA_WIN_H - 1] if rs <= k0 < rs + kh else neg
                b = hi[k0 + 1 - r + NA_WIN_H - 1] if rs <= k0 + 1 < rs + kh else neg
                bias_ref[i, qr * GRID_W:(qr + 1) * GRID_W, m * LANES:(m + 1) * LANES] = (
                    jnp.where(l < GRID_W, a, b))


def _na_body(q_ref, k_ref, v_ref, tab_ref, o_ref, bias_ref, vone_ref, *, rows):
    tq = NA_QROWS * GRID_W
    tk = NA_KROWS * GRID_W

    @pl.when(pl.program_id(1) == 0)
    def _():
        _na_build_bias(tab_ref, bias_ref, rows)

    vone_ref[:, :LANES] = v_ref[...]
    vone_ref[:, LANES:] = jnp.ones(v_ref.shape, BF16)

    for i in range(rows // NA_QROWS):
        ws = _na_window_start(i, rows) * GRID_W
        k = k_ref[ws:ws + tk, :]
        v_ones = vone_ref[ws:ws + tk, :]
        for c in range(tq // ATTN_SUB):
            lo = i * tq + c * ATTN_SUB
            s = _dot_t(q_ref[lo:lo + ATTN_SUB, :], k) + bias_ref[i, c * ATTN_SUB:(c + 1) * ATTN_SUB, :]
            o_ref[lo:lo + ATTN_SUB, :] = _softmax_pv(s, v_ones).astype(BF16)


def _na_bias_table(rpb):
    H, nd, nc = rpb.shape
    half = nc // 2
    fill = jnp.full((H, nd, LANES - nc), NEG, rpb.dtype)
    tab = jnp.concatenate([rpb[:, :, half:], fill, rpb[:, :, :half]], axis=2)
    return jnp.concatenate([tab, jnp.full((H, 16 - nd, LANES), NEG, rpb.dtype)], axis=1)


def _na_call(q, k, v, tab):
    B, S, _ = q.shape
    rows = S // GRID_W
    hb = lambda: pl.BlockSpec((None, S, LANES), lambda h, b: (b, 0, h))
    return pl.pallas_call(
        functools.partial(_na_body, rows=rows),
        grid=(NA_HEADS, B),
        in_specs=[hb(), hb(), hb(),
                  pl.BlockSpec((None,) + tab.shape[1:], lambda h, b: (h, 0, 0))],
        out_specs=hb(),
        out_shape=jax.ShapeDtypeStruct((B, S, NA_WIDTH), BF16),
        scratch_shapes=[pltpu.VMEM((rows // NA_QROWS, NA_QROWS * GRID_W, NA_KROWS * GRID_W), F32),
                        pltpu.VMEM((S, 2 * LANES), BF16)],
        compiler_params=_cparams(("arbitrary", "arbitrary")),
        name="na_attn",
    )(q, k, v, tab)


ATTN_SUB = 256


def _softmax_pv(s, v_ones):
    m = jnp.max(s, axis=-1, keepdims=True)
    e = jnp.exp2(s - m).astype(BF16)
    ol = jnp.dot(e, v_ones, preferred_element_type=F32)
    return ol[:, :LANES] / ol[:, LANES:]


def _mla_body(qn_ref, qpe_ref, kn_ref, kpe_ref, v_ref, o_ref, kcat_ref, vone_ref):
    @pl.when(pl.program_id(2) == 0)
    def _():
        kcat_ref[:, :LANES] = kn_ref[...]
        kcat_ref[:, LANES:] = kpe_ref[...]
        vone_ref[:, :LANES] = v_ref[...]
        vone_ref[:, LANES:] = jnp.ones(v_ref.shape, BF16)

    for c in range(qn_ref.shape[0] // ATTN_SUB):
        rows = slice(c * ATTN_SUB, (c + 1) * ATTN_SUB)
        q = jnp.concatenate([qn_ref[rows, :], qpe_ref[rows, :]], axis=1)
        s = _dot_t(q, kcat_ref[...])
        o_ref[rows, :] = _softmax_pv(s, vone_ref[...]).astype(BF16)


def _mla_call(qn, qpe, kn, kpe, v, tq):
    B, S, _ = qn.shape
    qb = lambda: pl.BlockSpec((None, tq, LANES), lambda b, h, i: (b, i, h))
    kb = lambda: pl.BlockSpec((None, S, LANES), lambda b, h, i: (b, 0, h))
    return pl.pallas_call(
        _mla_body,
        grid=(B, MLA_HEADS, S // tq),
        in_specs=[qb(), qb(), kb(), pl.BlockSpec((None, S, LANES), lambda b, h, i: (b, 0, 0)), kb()],
        out_specs=qb(),
        out_shape=jax.ShapeDtypeStruct((B, S, MLA_WIDTH), BF16),
        scratch_shapes=[pltpu.VMEM((S, 2 * LANES), BF16), pltpu.VMEM((S, 2 * LANES), BF16)],
        compiler_params=_cparams(("parallel", "arbitrary", "arbitrary")),
        name="mla_attn",
    )(qn, qpe, kn, kpe, v)


def _out_body(na_ref, mla_ref, x_ref, gna_ref, gmla_ref, wout_ref, g2_ref, wr_ref,
              h_ref, hn_ref, aff_ref):
    a = na_ref[...].astype(F32)
    an = (a * _rms(a, NA_WIDTH) * gna_ref[...]).astype(BF16)
    b = mla_ref[...].astype(F32)
    bn = (b * _rms(b, MLA_WIDTH) * gmla_ref[...]).astype(BF16)
    h = x_ref[...]
    h = h + jnp.dot(an, wout_ref[:NA_WIDTH, :], preferred_element_type=F32)
    h = h + jnp.dot(bn, wout_ref[NA_WIDTH:, :], preferred_element_type=F32)
    h_ref[...] = h
    hn = h * _rms(h, D_MODEL) * g2_ref[...]
    hn_ref[...] = hn
    logits = jnp.dot(hn.astype(BF16), wr_ref[...], preferred_element_type=F32)
    lane = lax.broadcasted_iota(jnp.int32, logits.shape, 1)
    logits = jnp.where(lane < N_EXPERTS, logits, NEG)
    m = jnp.max(logits, axis=-1, keepdims=True)
    e = jnp.exp(logits - m)
    aff = e / jnp.sum(e, axis=-1, keepdims=True)
    aff_ref[...] = aff.T[:N_EXPERTS, :]


def _out_call(na, mla, x, p, tm):
    B, S, D = x.shape
    row = lambda w: pl.BlockSpec((None, tm, w), lambda b, i: (b, i, 0))
    nblk = S // tm
    return pl.pallas_call(
        _out_body,
        grid=(B, nblk),
        in_specs=[row(NA_WIDTH), row(MLA_WIDTH), row(D), _resident((1, NA_WIDTH)),
                  _resident((1, MLA_WIDTH)), _resident((NA_WIDTH + MLA_WIDTH, D)),
                  _resident((1, D)), _resident((D, LANES))],
        out_specs=[row(D), row(D), pl.BlockSpec((N_EXPERTS, tm), lambda b, i: (0, b * nblk + i))],
        out_shape=[jax.ShapeDtypeStruct((B, S, D), F32), jax.ShapeDtypeStruct((B, S, D), F32),
                   jax.ShapeDtypeStruct((N_EXPERTS, B * S), F32)],
        compiler_params=_cparams(("parallel", "arbitrary")),
        name="out_proj",
    )(na, mla, x, p['gna'], p['gmla'], p['wout'], p['g2'], p['wr'])


def _tri(n, strict, upper):
    r = lax.broadcasted_iota(jnp.int32, (n, n), 0)
    c = lax.broadcasted_iota(jnp.int32, (n, n), 1)
    if upper:
        m = (r < c) if strict else (r <= c)
    else:
        m = (c < r) if strict else (c <= r)
    return jnp.where(m, 1.0, 0.0).astype(BF16)


def _dot_t(a, b):
    return lax.dot_general(a, b, (((1,), (1,)), ((), ())), preferred_element_type=F32)


def _topk_body(aff_ref, idx_ref, gate_ref, sel_ref, cum_ref, *, nb, cap):
    E = N_EXPERTS
    aff = aff_ref[...]
    bits = pltpu.bitcast(aff, jnp.int32)

    def count(mask):
        c = jnp.sum(jnp.where(mask, 1.0, 0.0), axis=1, keepdims=True)
        return jnp.sum(c, axis=2, keepdims=True)

    def search(b, t):
        cand = t | jnp.left_shift(jnp.int32(1), 30 - b)
        return jnp.where(count(bits >= cand) >= cap, cand, t)

    t = lax.fori_loop(0, 31, search, jnp.zeros((E, 1, 1), jnp.int32))

    upper_incl = _tri(LANES, strict=False, upper=True)
    lower_strict = jnp.broadcast_to(_tri(nb, strict=True, upper=False)[None], (E, nb, nb))

    def cumsum_incl(m):
        mb = m.astype(BF16)
        within = jnp.dot(mb.reshape(E * nb, LANES), upper_incl,
                         preferred_element_type=F32).reshape(E, nb, LANES)
        before = lax.dot_general(lower_strict, mb, (((2,), (1,)), ((0,), (0,))),
                                 preferred_element_type=F32)
        return within + jnp.sum(before, axis=2, keepdims=True)

    gt = bits > t
    eq = jnp.where(bits == t, 1.0, 0.0)
    need = cap - count(gt)
    eq_before = cumsum_incl(eq) - eq
    sel = jnp.where(gt | ((eq > 0.5) & (eq_before < need)), 1.0, 0.0)
    sel_ref[...] = sel
    cum_ref[...] = cumsum_incl(sel)

    ones_rows = jnp.ones((8, LANES), BF16)
    lane0 = jnp.where(lax.broadcasted_iota(jnp.int32, (8, LANES), 1) == 0, 1.0, 0.0).astype(BF16)
    ones_nb = jnp.ones((nb, LANES), BF16)
    ones_sq = jnp.ones((LANES, LANES), BF16)
    upper_strict_nb = _tri(nb, strict=True, upper=True)
    slot_nb = lax.broadcasted_iota(jnp.int32, (cap, nb), 0).astype(F32)
    slot = lax.broadcasted_iota(jnp.int32, (cap, LANES), 0).astype(F32)
    lane = lax.broadcasted_iota(jnp.int32, (cap, LANES), 1).astype(F32)

    def compact(e, carry):
        m = sel_ref[e].astype(BF16)
        c = cum_ref[e]
        row_tot = _dot_t(ones_rows, m)
        row_off = jnp.dot(row_tot.astype(BF16), upper_strict_nb, preferred_element_type=F32)
        row_end = row_off + row_tot
        off_b = row_off[0:1, :]
        end_b = row_end[0:1, :]
        onehot = jnp.where((off_b <= slot_nb) & (slot_nb < end_b), 1.0, 0.0).astype(BF16)
        rows_before = jnp.where(end_b <= slot_nb, 1.0, 0.0).astype(BF16)
        row_idx = jnp.dot(rows_before, ones_nb, preferred_element_type=F32)
        c_hi = jnp.floor(c * (1.0 / LANES))
        c_lo = c - c_hi * LANES
        c_row = (jnp.dot(onehot, c_hi.astype(BF16), preferred_element_type=F32) * LANES
                 + jnp.dot(onehot, c_lo.astype(BF16), preferred_element_type=F32))
        before = jnp.where(c_row <= slot, 1.0, 0.0).astype(BF16)
        lane_idx = jnp.dot(before, ones_sq, preferred_element_type=F32)
        tok = (_dot_t(lane0, row_idx.astype(BF16)) * LANES
               + _dot_t(lane0, lane_idx.astype(BF16)))
        idx_ref[e] = tok.astype(jnp.int32)
        a = aff_ref[e]
        a1 = a.astype(BF16)
        r1 = a - a1.astype(F32)
        a2 = r1.astype(BF16)
        a3 = (r1 - a2.astype(F32)).astype(BF16)
        a_row = (jnp.dot(onehot, a1, preferred_element_type=F32)
                 + jnp.dot(onehot, a2, preferred_element_type=F32)
                 + jnp.dot(onehot, a3, preferred_element_type=F32))
        gate_ref[e] = jnp.sum(jnp.where(lane == lane_idx, a_row, 0.0), axis=1, keepdims=True)
        return carry

    lax.fori_loop(0, E, compact, 0)


def _topk_call(aff_t):
    E, N = aff_t.shape
    nb = N // LANES
    cap = EC_CAPACITY * N // N_EXPERTS
    idx, gates = pl.pallas_call(
        functools.partial(_topk_body, nb=nb, cap=cap),
        out_shape=[jax.ShapeDtypeStruct((E, 8, cap), jnp.int32),
                   jax.ShapeDtypeStruct((E, cap, 1), F32)],
        scratch_shapes=[pltpu.VMEM((E, nb, LANES), F32), pltpu.VMEM((E, nb, LANES), F32)],
        compiler_params=pltpu.CompilerParams(vmem_limit_bytes=VMEM_LIMIT),
        name="ec_topk",
    )(aff_t.reshape(E, nb, LANES))
    return idx[:, 0, :], gates


FFN_ROWS = 1024
FFN_FC = 256
FFN_DC = 512


def _ffn_body(idx_ref, hn_hbm, yin_hbm, gate_ref, wg_ref, wu_ref, wd_ref, y_hbm,
              xrows_ref, yrows_ref, xe_ref, acc_ref, sem, *, cap, rows):
    del yin_hbm
    e, r, f = pl.program_id(0), pl.program_id(1), pl.program_id(2)
    nr, nf = pl.num_programs(1), pl.num_programs(2)
    block = e * nr + r
    base = block * rows
    last_f = f == nf - 1

    def x_gather(first_slot, s):
        return pltpu.make_async_copy(hn_hbm.at[idx_ref[first_slot + s]], xrows_ref.at[s], sem.at[0])

    def y_gather(s):
        return pltpu.make_async_copy(y_hbm.at[idx_ref[base + s]], yrows_ref.at[s], sem.at[1])

    def scatter(s):
        return pltpu.make_async_copy(yrows_ref.at[s], y_hbm.at[idx_ref[base + s]], sem.at[2])

    def for_rows(fn):
        def body(s, c):
            fn(s)
            return c
        lax.fori_loop(0, rows, body, 0, unroll=8)

    @pl.when((f == 0) & (block == 0))
    def _():
        for_rows(lambda s: x_gather(base, s).start())

    @pl.when(f == 0)
    def _():
        for_rows(lambda s: x_gather(base, s).wait())
        xe_ref[...] = xrows_ref[...].astype(BF16)

    @pl.when((f == 1) & (block + 1 < pl.num_programs(0) * nr))
    def _():
        for_rows(lambda s: x_gather(base + rows, s).start())

    @pl.when(last_f)
    def _():
        for_rows(lambda s: y_gather(s).start())

    x = xe_ref[...]
    g = jnp.dot(x, wg_ref[...].astype(BF16), preferred_element_type=F32)
    u = jnp.dot(x, wu_ref[...].astype(BF16), preferred_element_type=F32)
    hid = (jax.nn.silu(g) * u).astype(BF16)
    wd = wd_ref[...].astype(BF16)

    def down(accumulate):
        for c in range(D_MODEL // FFN_DC):
            cs = slice(c * FFN_DC, (c + 1) * FFN_DC)
            part = jnp.dot(hid, wd[:, cs], preferred_element_type=F32)
            acc_ref[:, cs] = acc_ref[:, cs] + part if accumulate else part

    pl.when(f == 0)(lambda: down(False))
    pl.when(f != 0)(lambda: down(True))

    @pl.when(last_f)
    def _():
        for_rows(lambda s: y_gather(s).wait())
        yrows_ref[...] = yrows_ref[...] + acc_ref[...] * gate_ref[...]
        for_rows(lambda s: scatter(s).start())
        for_rows(lambda s: scatter(s).wait())


def _ffn_call(idx, gates, hn, h, w_gate, w_up, w_down):
    E, cap = idx.shape
    N, D = hn.shape
    F = w_gate.shape[2]
    rows = min(FFN_ROWS, cap)
    grid_spec = pltpu.PrefetchScalarGridSpec(
        num_scalar_prefetch=1,
        grid=(E, cap // rows, F // FFN_FC),
        in_specs=[
            pl.BlockSpec(memory_space=pl.ANY),
            pl.BlockSpec(memory_space=pl.ANY),
            pl.BlockSpec((None, rows, 1), lambda e, r, f, idx: (e, r, 0)),
            pl.BlockSpec((None, D, FFN_FC), lambda e, r, f, idx: (e, 0, f)),
            pl.BlockSpec((None, D, FFN_FC), lambda e, r, f, idx: (e, 0, f)),
            pl.BlockSpec((None, FFN_FC, D), lambda e, r, f, idx: (e, f, 0)),
        ],
        out_specs=pl.BlockSpec(memory_space=pl.ANY),
        scratch_shapes=[pltpu.VMEM((rows, D), F32), pltpu.VMEM((rows, D), F32),
                        pltpu.VMEM((rows, D), BF16), pltpu.VMEM((rows, D), F32),
                        pltpu.SemaphoreType.DMA((3,))],
    )
    assert F // FFN_FC >= 2, "the next block's row gather is issued from hidden chunk 1"
    return pl.pallas_call(
        functools.partial(_ffn_body, cap=cap, rows=rows),
        grid_spec=grid_spec,
        out_shape=jax.ShapeDtypeStruct((N, D), F32),
        input_output_aliases={2: 0},
        compiler_params=_cparams(("arbitrary", "arbitrary", "arbitrary")),
        name="ec_ffn",
    )(idx.reshape(-1), hn, h, gates, w_gate, w_up, w_down)


def _rope_tables(S):
    t = np.arange(S)
    row = (t // GRID_W).astype(np.float32)
    col = (t % GRID_W).astype(np.float32)
    nf = ROPE_DIM // 4
    inv = (1.0 / (ROPE_BASE ** (jnp.arange(nf, dtype=F32) / nf)))
    ar = jnp.asarray(row)[:, None] * inv
    ac = jnp.asarray(col)[:, None] * inv
    cr, sr, cc, sc = jnp.cos(ar), jnp.sin(ar), jnp.cos(ac), jnp.sin(ac)
    z = jnp.zeros((S, LANES - ROPE_DIM), F32)
    cos = jnp.concatenate([cr, cr, cc, cc, z], axis=1)
    sin = jnp.concatenate([-sr, sr, -sc, sc, z], axis=1)
    return cos, sin


def _pad_lanes(v, fill=0.0):
    return jnp.concatenate([v, jnp.full((LANES - v.shape[0],), fill, v.dtype)])[None, :]


def _prepare(S, norm1_g, w_in, na_q_norm, na_k_norm, na_rpb, q_a_norm, w_q_up, kv_a_norm, w_kv_up,
             mla_qn_norm, mla_qr_norm, mla_kn_norm, mla_kr_norm, na_out_norm, mla_out_norm, w_out,
             norm2_g, w_router):
    p = {}
    p['g1'] = norm1_g[None, :]
    p['win'] = jnp.concatenate(
        [w_in, jnp.zeros((D_MODEL, ROPE_PAD - ROPE_DIM), w_in.dtype)], axis=1).astype(BF16)
    wq = w_q_up.reshape(Q_LORA, MLA_HEADS, QK_DIM)
    wq_nope = wq[:, :, :NOPE_DIM].reshape(Q_LORA, MLA_HEADS * NOPE_DIM)
    wq_pe = jnp.pad(wq[:, :, NOPE_DIM:], ((0, 0), (0, 0), (0, LANES - ROPE_DIM)))
    p['wq'] = jnp.concatenate([wq_nope, wq_pe.reshape(Q_LORA, MLA_HEADS * LANES)], axis=1).astype(BF16)
    wkv = w_kv_up.reshape(KV_LORA, MLA_HEADS, NOPE_DIM + V_DIM)
    p['wkv'] = jnp.concatenate([wkv[:, :, :NOPE_DIM].reshape(KV_LORA, -1),
                                wkv[:, :, NOPE_DIM:].reshape(KV_LORA, -1)], axis=1).astype(BF16)
    p['gq'] = (na_q_norm * (NA_HEAD_DIM ** -0.5 * LOG2E))[None, :]
    p['gk'] = na_k_norm[None, :]
    p['gqa'] = q_a_norm[None, :]
    p['gkva'] = kv_a_norm[None, :]
    p['gkr'] = _pad_lanes(mla_kr_norm)
    p['gqn'] = (mla_qn_norm * (QK_DIM ** -0.5 * LOG2E))[None, :]
    p['gqr'] = _pad_lanes(mla_qr_norm * (QK_DIM ** -0.5 * LOG2E))
    p['gkn'] = mla_kn_norm[None, :]
    p['cos'], p['sin'] = _rope_tables(S)
    p['bias'] = _na_bias_table(na_rpb * LOG2E)
    p['gna'] = na_out_norm[None, :]
    p['gmla'] = mla_out_norm[None, :]
    p['wout'] = w_out.astype(BF16)
    p['g2'] = norm2_g[None, :]
    p['wr'] = jnp.pad(w_router, ((0, 0), (0, LANES - N_EXPERTS))).astype(BF16)
    return p


def _attention_block(x, p):
    qna, kna, vna, qn, qpe, kn, vm, kpe = _proj_call(x, p, tm=512)
    out_na = _na_call(qna, kna, vna, p['bias'])
    out_mla = _mla_call(qn, qpe, kn, kpe, vm, tq=x.shape[1])
    return _out_call(out_na, out_mla, x, p, tm=512)


def _layer(x, p, w_gate, w_up, w_down):
    B, S, D = x.shape
    h, hn, aff_t = _attention_block(x, p)
    idx, gates = _topk_call(aff_t)
    y = _ffn_call(idx, gates, hn.reshape(B * S, D), h.reshape(B * S, D), w_gate, w_up, w_down)
    return y.reshape(B, S, D)


def kernel(x_prompt, x_sample, norm1_g, w_in, na_q_norm, na_k_norm, na_rpb, q_a_norm, w_q_up, kv_a_norm, w_kv_up, mla_qn_norm, mla_qr_norm, mla_kn_norm, mla_kr_norm, na_out_norm, mla_out_norm, w_out, norm2_g, w_router, w_gate, w_up, w_down):
    hp, hs = x_prompt, x_sample
    for l in range(norm1_g.shape[0]):
        p = _prepare(hp.shape[1], norm1_g[l], w_in[l], na_q_norm[l], na_k_norm[l], na_rpb[l],
                     q_a_norm[l], w_q_up[l], kv_a_norm[l], w_kv_up[l], mla_qn_norm[l],
                     mla_qr_norm[l], mla_kn_norm[l], mla_kr_norm[l], na_out_norm[l],
                     mla_out_norm[l], w_out[l], norm2_g[l], w_router[l])
        hp = _layer(hp, p, w_gate[l], w_up[l], w_down[l])
        hs = _layer(hs, p, w_gate[l], w_up[l], w_down[l])
    return (hp, hs)
```

```python
import functools

import jax
import jax.numpy as jnp
import numpy as np
from jax import lax
from jax.experimental import pallas as pl
from jax.experimental.pallas import tpu as pltpu

D_MODEL = 2048
GRID_W = 64
NA_HEADS = 8
NA_HEAD_DIM = 128
NA_WIN_H = 8
NA_WIN_W = 16
MLA_HEADS = 8
Q_LORA = 512
KV_LORA = 512
NOPE_DIM = 128
ROPE_DIM = 64
V_DIM = 128
ROPE_BASE = 10000.0
N_EXPERTS = 16
EC_CAPACITY = 2
EXPERT_FF = 1024
EPS = 1e-6

NA_WIDTH = NA_HEADS * NA_HEAD_DIM
MLA_WIDTH = MLA_HEADS * V_DIM
QK_DIM = NOPE_DIM + ROPE_DIM
LANES = 128
ROPE_PAD = LANES
P_PAD = 3 * NA_WIDTH + Q_LORA + KV_LORA + ROPE_PAD
NEG = -1e30
LOG2E = 1.4426950408889634

BF16 = jnp.bfloat16
F32 = jnp.float32

VMEM_LIMIT = 56 * 1024 * 1024


def _cparams(sem):
    return pltpu.CompilerParams(dimension_semantics=sem, vmem_limit_bytes=VMEM_LIMIT)


def _resident(shape):
    nd = len(shape)
    return pl.BlockSpec(shape, lambda *_: (0,) * nd, pipeline_mode=pl.Buffered(1))


def _rms(a, width):
    return lax.rsqrt(jnp.sum(a * a, axis=-1, keepdims=True) * (1.0 / width) + EPS)


def _rope(a, cos, sin):
    lane = lax.broadcasted_iota(jnp.int32, a.shape, 1)
    up = pltpu.roll(a, LANES - 16, axis=1)
    dn = pltpu.roll(a, 16, axis=1)
    partner = jnp.where((lane % 32) < 16, up, dn)
    return a * cos + partner * sin


def _proj_body(x_ref, g1_ref, win_ref, wq_ref, wkv_ref, gq_ref, gk_ref, gqa_ref, gkva_ref,
               gkr_ref, gqn_ref, gqr_ref, gkn_ref, cos_ref, sin_ref,
               qna_ref, kna_ref, vna_ref, qn_ref, qpe_ref, kn_ref, vm_ref, kpe_ref):
    x = x_ref[...]
    xn = (x * _rms(x, D_MODEL) * g1_ref[...]).astype(BF16)
    cos = cos_ref[...]
    sin = sin_ref[...]

    def proj(lo, hi):
        return jnp.dot(xn, win_ref[:, lo:hi], preferred_element_type=F32)

    def heads_to(ref, a, g, width, base, rope=False):
        for h in range(a.shape[1] // LANES):
            ah = a[:, h * LANES:(h + 1) * LANES]
            y = ah * _rms(ah, width) * g
            if rope:
                y = _rope(y, cos, sin)
            ref[:, base + h * LANES: base + (h + 1) * LANES] = y.astype(BF16)

    half = NA_WIDTH // 2
    for c in range(2):
        heads_to(qna_ref, proj(c * half, (c + 1) * half), gq_ref[...], NA_HEAD_DIM, c * half)
    for c in range(2):
        heads_to(kna_ref, proj(NA_WIDTH + c * half, NA_WIDTH + (c + 1) * half), gk_ref[...],
                 NA_HEAD_DIM, c * half)
    for c in range(2):
        vna_ref[:, c * half:(c + 1) * half] = proj(2 * NA_WIDTH + c * half,
                                                   2 * NA_WIDTH + (c + 1) * half).astype(BF16)

    o = 3 * NA_WIDTH
    ql = proj(o, o + Q_LORA)
    qln = (ql * _rms(ql, Q_LORA) * gqa_ref[...]).astype(BF16)
    kvl = proj(o + Q_LORA, o + Q_LORA + KV_LORA)
    kvln = (kvl * _rms(kvl, KV_LORA) * gkva_ref[...]).astype(BF16)
    kp = proj(o + Q_LORA + KV_LORA, P_PAD)
    heads_to(kpe_ref, kp, gkr_ref[...], ROPE_DIM, 0, rope=True)

    nh = MLA_HEADS * LANES
    for c in range(2):
        a = jnp.dot(qln, wq_ref[:, c * half:(c + 1) * half], preferred_element_type=F32)
        heads_to(qn_ref, a, gqn_ref[...], NOPE_DIM, c * half)
    for c in range(2):
        a = jnp.dot(qln, wq_ref[:, nh + c * half: nh + (c + 1) * half], preferred_element_type=F32)
        heads_to(qpe_ref, a, gqr_ref[...], ROPE_DIM, c * half, rope=True)
    for c in range(2):
        a = jnp.dot(kvln, wkv_ref[:, c * half:(c + 1) * half], preferred_element_type=F32)
        heads_to(kn_ref, a, gkn_ref[...], NOPE_DIM, c * half)
    for c in range(2):
        vm_ref[:, c * half:(c + 1) * half] = jnp.dot(
            kvln, wkv_ref[:, nh + c * half: nh + (c + 1) * half],
            preferred_element_type=F32).astype(BF16)


def _proj_call(x, p, tm):
    B, S, D = x.shape
    grid = (B, S // tm)
    row = lambda w: pl.BlockSpec((None, tm, w), lambda b, i: (b, i, 0))
    tab = pl.BlockSpec((tm, LANES), lambda b, i: (i, 0))
    vec = lambda w: _resident((1, w))
    out_w = (NA_WIDTH, NA_WIDTH, NA_WIDTH, MLA_HEADS * LANES, MLA_HEADS * LANES,
             MLA_HEADS * LANES, MLA_WIDTH, LANES)
    return pl.pallas_call(
        _proj_body,
        grid=grid,
        in_specs=[row(D), vec(D), _resident((D, P_PAD)), _resident(p['wq'].shape),
                  _resident(p['wkv'].shape), vec(LANES), vec(LANES), vec(Q_LORA), vec(KV_LORA),
                  vec(LANES), vec(LANES), vec(LANES), vec(LANES), tab, tab],
        out_specs=[row(w) for w in out_w],
        out_shape=[jax.ShapeDtypeStruct((B, S, w), BF16) for w in out_w],
        compiler_params=_cparams(("parallel", "arbitrary")),
        name="in_proj",
    )(x, p['g1'], p['win'], p['wq'], p['wkv'], p['gq'], p['gk'], p['gqa'], p['gkva'],
      p['gkr'], p['gqn'], p['gqr'], p['gkn'], p['cos'], p['sin'])


NA_QROWS = 8
NA_KROWS = 16


def _na_window_start(i, rows):
    return int(np.clip(NA_QROWS * i - NA_WIN_H // 2, 0, rows - NA_KROWS))


def _na_build_bias(tab_ref, bias_ref, rows):
    kh = min(NA_WIN_H, rows)
    shape = (GRID_W, LANES)
    c = lax.broadcasted_iota(jnp.int32, shape, 0)
    l = lax.broadcasted_iota(jnp.int32, shape, 1)
    cs = jnp.clip(c - NA_WIN_W // 2, 0, GRID_W - NA_WIN_W)
    in_lo = (l >= cs) & (l < cs + NA_WIN_W)
    in_hi = (l - GRID_W >= cs) & (l - GRID_W < cs + NA_WIN_W)
    neg = jnp.full(shape, NEG, F32)
    lo, hi = [], []
    for d in range(2 * NA_WIN_H - 1):
        base = jnp.broadcast_to(tab_ref[d:d + 1, :], shape)
        lo.append(jnp.where(in_lo, pltpu.roll(base, 0, 1, stride=1, stride_axis=0), NEG))
        hi.append(jnp.where(in_hi, pltpu.roll(base, GRID_W, 1, stride=1, stride_axis=0), NEG))
    for i in range(rows // NA_QROWS):
        ws = _na_window_start(i, rows)
        for qr in range(NA_QROWS):
            r = NA_QROWS * i + qr
            rs = int(np.clip(r - kh // 2, 0, rows - kh))
            for m in range(NA_KROWS // 2):
                k0 = ws + 2 * m
                a = lo[k0 - r + NA_WIN_H - 1] if rs <= k0 < rs + kh else neg
                b = hi[k0 + 1 - r + NA_WIN_H - 1] if rs <= k0 + 1 < rs + kh else neg
                bias_ref[i, qr * GRID_W:(qr + 1) * GRID_W, m * LANES:(m + 1) * LANES] = (
                    jnp.where(l < GRID_W, a, b))


def _na_body(q_ref, k_ref, v_ref, tab_ref, o_ref, bias_ref, vone_ref, *, rows):
    tq = NA_QROWS * GRID_W
    tk = NA_KROWS * GRID_W

    @pl.when(pl.program_id(1) == 0)
    def _():
        _na_build_bias(tab_ref, bias_ref, rows)

    vone_ref[:, :LANES] = v_ref[...]
    vone_ref[:, LANES:] = jnp.ones(v_ref.shape, BF16)

    for i in range(rows // NA_QROWS):
        ws = _na_window_start(i, rows) * GRID_W
        k = k_ref[ws:ws + tk, :]
        v_ones = vone_ref[ws:ws + tk, :]
        for c in range(tq // ATTN_SUB):
            lo = i * tq + c * ATTN_SUB
            s = _dot_t(q_ref[lo:lo + ATTN_SUB, :], k) + bias_ref[i, c * ATTN_SUB:(c + 1) * ATTN_SUB, :]
            o_ref[lo:lo + ATTN_SUB, :] = _softmax_pv(s, v_ones).astype(BF16)


def _na_bias_table(rpb):
    H, nd, nc = rpb.shape
    half = nc // 2
    fill = jnp.full((H, nd, LANES - nc), NEG, rpb.dtype)
    tab = jnp.concatenate([rpb[:, :, half:], fill, rpb[:, :, :half]], axis=2)
    return jnp.concatenate([tab, jnp.full((H, 16 - nd, LANES), NEG, rpb.dtype)], axis=1)


def _na_call(q, k, v, tab):
    B, S, _ = q.shape
    rows = S // GRID_W
    hb = lambda: pl.BlockSpec((None, S, LANES), lambda h, b: (b, 0, h))
    return pl.pallas_call(
        functools.partial(_na_body, rows=rows),
        grid=(NA_HEADS, B),
        in_specs=[hb(), hb(), hb(),
                  pl.BlockSpec((None,) + tab.shape[1:], lambda h, b: (h, 0, 0))],
        out_specs=hb(),
        out_shape=jax.ShapeDtypeStruct((B, S, NA_WIDTH), BF16),
        scratch_shapes=[pltpu.VMEM((rows // NA_QROWS, NA_QROWS * GRID_W, NA_KROWS * GRID_W), F32),
                        pltpu.VMEM((S, 2 * LANES), BF16)],
        compiler_params=_cparams(("arbitrary", "arbitrary")),
        name="na_attn",
    )(q, k, v, tab)


ATTN_SUB = 256


def _softmax_pv(s, v_ones):
    m = jnp.max(s, axis=-1, keepdims=True)
    e = jnp.exp2(s - m).astype(BF16)
    ol = jnp.dot(e, v_ones, preferred_element_type=F32)
    return ol[:, :LANES] / ol[:, LANES:]


def _mla_body(qn_ref, qpe_ref, kn_ref, kpe_ref, v_ref, o_ref, kcat_ref, vone_ref):
    @pl.when(pl.program_id(2) == 0)
    def _():
        kcat_ref[:, :LANES] = kn_ref[...]
        kcat_ref[:, LANES:] = kpe_ref[...]
        vone_ref[:, :LANES] = v_ref[...]
        vone_ref[:, LANES:] = jnp.ones(v_ref.shape, BF16)

    for c in range(qn_ref.shape[0] // ATTN_SUB):
        rows = slice(c * ATTN_SUB, (c + 1) * ATTN_SUB)
        q = jnp.concatenate([qn_ref[rows, :], qpe_ref[rows, :]], axis=1)
        s = _dot_t(q, kcat_ref[...])
        o_ref[rows, :] = _softmax_pv(s, vone_ref[...]).astype(BF16)


def _mla_call(qn, qpe, kn, kpe, v, tq):
    B, S, _ = qn.shape
    qb = lambda: pl.BlockSpec((None, tq, LANES), lambda b, h, i: (b, i, h))
    kb = lambda: pl.BlockSpec((None, S, LANES), lambda b, h, i: (b, 0, h))
    return pl.pallas_call(
        _mla_body,
        grid=(B, MLA_HEADS, S // tq),
        in_specs=[qb(), qb(), kb(), pl.BlockSpec((None, S, LANES), lambda b, h, i: (b, 0, 0)), kb()],
        out_specs=qb(),
        out_shape=jax.ShapeDtypeStruct((B, S, MLA_WIDTH), BF16),
        scratch_shapes=[pltpu.VMEM((S, 2 * LANES), BF16), pltpu.VMEM((S, 2 * LANES), BF16)],
        compiler_params=_cparams(("parallel", "arbitrary", "arbitrary")),
        name="mla_attn",
    )(qn, qpe, kn, kpe, v)


TOK_ROWS = D_MODEL // LANES
TOK_PITCH = TOK_ROWS + 1


def _store_token_major(ref, val):
    t = val.shape[0]
    for a in range(TOK_ROWS):
        ref[pl.ds(a, t, stride=TOK_PITCH), :] = val[:, a * LANES:(a + 1) * LANES]


def _load_token_major(ref, t):
    return [ref[pl.ds(a, t, stride=TOK_PITCH), :] for a in range(TOK_ROWS)]


def _out_body(na_ref, mla_ref, x_ref, gna_ref, gmla_ref, wout_ref, g2_ref, wr_ref,
              h_ref, hn_ref, aff_ref):
    a = na_ref[...].astype(F32)
    an = (a * _rms(a, NA_WIDTH) * gna_ref[...]).astype(BF16)
    b = mla_ref[...].astype(F32)
    bn = (b * _rms(b, MLA_WIDTH) * gmla_ref[...]).astype(BF16)
    h = x_ref[...]
    h = h + jnp.dot(an, wout_ref[:NA_WIDTH, :], preferred_element_type=F32)
    h = h + jnp.dot(bn, wout_ref[NA_WIDTH:, :], preferred_element_type=F32)
    hn = h * _rms(h, D_MODEL) * g2_ref[...]
    _store_token_major(h_ref, h)
    _store_token_major(hn_ref, hn)
    logits = jnp.dot(hn.astype(BF16), wr_ref[...], preferred_element_type=F32)
    lane = lax.broadcasted_iota(jnp.int32, logits.shape, 1)
    logits = jnp.where(lane < N_EXPERTS, logits, NEG)
    m = jnp.max(logits, axis=-1, keepdims=True)
    e = jnp.exp(logits - m)
    aff = e / jnp.sum(e, axis=-1, keepdims=True)
    aff_ref[...] = aff.T[:N_EXPERTS, :]


def _out_call(na, mla, x, p, tm):
    B, S, D = x.shape
    row = lambda w: pl.BlockSpec((None, tm, w), lambda b, i: (b, i, 0))
    nblk = S // tm
    tokmaj = lambda: pl.BlockSpec((tm * TOK_PITCH, LANES), lambda b, i: (b * nblk + i, 0))
    return pl.pallas_call(
        _out_body,
        grid=(B, nblk),
        in_specs=[row(NA_WIDTH), row(MLA_WIDTH), row(D), _resident((1, NA_WIDTH)),
                  _resident((1, MLA_WIDTH)), _resident((NA_WIDTH + MLA_WIDTH, D)),
                  _resident((1, D)), _resident((D, LANES))],
        out_specs=[tokmaj(), tokmaj(), pl.BlockSpec((N_EXPERTS, tm), lambda b, i: (0, b * nblk + i))],
        out_shape=[jax.ShapeDtypeStruct((B * S * TOK_PITCH, LANES), F32),
                   jax.ShapeDtypeStruct((B * S * TOK_PITCH, LANES), F32),
                   jax.ShapeDtypeStruct((N_EXPERTS, B * S), F32)],
        compiler_params=_cparams(("parallel", "arbitrary")),
        name="out_proj",
    )(na, mla, x, p['gna'], p['gmla'], p['wout'], p['g2'], p['wr'])


def _tri(n, strict, upper):
    r = lax.broadcasted_iota(jnp.int32, (n, n), 0)
    c = lax.broadcasted_iota(jnp.int32, (n, n), 1)
    if upper:
        m = (r < c) if strict else (r <= c)
    else:
        m = (c < r) if strict else (c <= r)
    return jnp.where(m, 1.0, 0.0).astype(BF16)


def _dot_t(a, b):
    return lax.dot_general(a, b, (((1,), (1,)), ((), ())), preferred_element_type=F32)


def _topk_body(aff_ref, idx_ref, gate_ref, sel_ref, cum_ref, *, nb, cap):
    E = N_EXPERTS
    aff = aff_ref[...]
    bits = pltpu.bitcast(aff, jnp.int32)

    def count(mask):
        c = jnp.sum(jnp.where(mask, 1.0, 0.0), axis=1, keepdims=True)
        return jnp.sum(c, axis=2, keepdims=True)

    def search(b, t):
        cand = t | jnp.left_shift(jnp.int32(1), 30 - b)
        return jnp.where(count(bits >= cand) >= cap, cand, t)

    t = lax.fori_loop(0, 31, search, jnp.zeros((E, 1, 1), jnp.int32))

    upper_incl = _tri(LANES, strict=False, upper=True)
    lower_strict = jnp.broadcast_to(_tri(nb, strict=True, upper=False)[None], (E, nb, nb))

    def cumsum_incl(m):
        mb = m.astype(BF16)
        within = jnp.dot(mb.reshape(E * nb, LANES), upper_incl,
                         preferred_element_type=F32).reshape(E, nb, LANES)
        before = lax.dot_general(lower_strict, mb, (((2,), (1,)), ((0,), (0,))),
                                 preferred_element_type=F32)
        return within + jnp.sum(before, axis=2, keepdims=True)

    gt = bits > t
    eq = jnp.where(bits == t, 1.0, 0.0)
    need = cap - count(gt)
    eq_before = cumsum_incl(eq) - eq
    sel = jnp.where(gt | ((eq > 0.5) & (eq_before < need)), 1.0, 0.0)
    sel_ref[...] = sel
    cum_ref[...] = cumsum_incl(sel)

    ones_rows = jnp.ones((8, LANES), BF16)
    lane0 = jnp.where(lax.broadcasted_iota(jnp.int32, (8, LANES), 1) == 0, 1.0, 0.0).astype(BF16)
    ones_nb = jnp.ones((nb, LANES), BF16)
    ones_sq = jnp.ones((LANES, LANES), BF16)
    upper_strict_nb = _tri(nb, strict=True, upper=True)
    slot_nb = lax.broadcasted_iota(jnp.int32, (cap, nb), 0).astype(F32)
    slot = lax.broadcasted_iota(jnp.int32, (cap, LANES), 0).astype(F32)
    lane = lax.broadcasted_iota(jnp.int32, (cap, LANES), 1).astype(F32)

    def compact(e, carry):
        m = sel_ref[e].astype(BF16)
        c = cum_ref[e]
        row_tot = _dot_t(ones_rows, m)
        row_off = jnp.dot(row_tot.astype(BF16), upper_strict_nb, preferred_element_type=F32)
        row_end = row_off + row_tot
        off_b = row_off[0:1, :]
        end_b = row_end[0:1, :]
        onehot = jnp.where((off_b <= slot_nb) & (slot_nb < end_b), 1.0, 0.0).astype(BF16)
        rows_before = jnp.where(end_b <= slot_nb, 1.0, 0.0).astype(BF16)
        row_idx = jnp.dot(rows_before, ones_nb, preferred_element_type=F32)
        c_hi = jnp.floor(c * (1.0 / LANES))
        c_lo = c - c_hi * LANES
        c_row = (jnp.dot(onehot, c_hi.astype(BF16), preferred_element_type=F32) * LANES
                 + jnp.dot(onehot, c_lo.astype(BF16), preferred_element_type=F32))
        before = jnp.where(c_row <= slot, 1.0, 0.0).astype(BF16)
        lane_idx = jnp.dot(before, ones_sq, preferred_element_type=F32)
        tok = (_dot_t(lane0, row_idx.astype(BF16)) * LANES
               + _dot_t(lane0, lane_idx.astype(BF16)))
        idx_ref[e] = tok.astype(jnp.int32)
        a = aff_ref[e]
        a1 = a.astype(BF16)
        r1 = a - a1.astype(F32)
        a2 = r1.astype(BF16)
        a3 = (r1 - a2.astype(F32)).astype(BF16)
        a_row = (jnp.dot(onehot, a1, preferred_element_type=F32)
                 + jnp.dot(onehot, a2, preferred_element_type=F32)
                 + jnp.dot(onehot, a3, preferred_element_type=F32))
        gate_ref[e] = jnp.sum(jnp.where(lane == lane_idx, a_row, 0.0), axis=1, keepdims=True)
        return carry

    lax.fori_loop(0, E, compact, 0)


def _topk_call(aff_t):
    E, N = aff_t.shape
    nb = N // LANES
    cap = EC_CAPACITY * N // N_EXPERTS
    idx, gates = pl.pallas_call(
        functools.partial(_topk_body, nb=nb, cap=cap),
        out_shape=[jax.ShapeDtypeStruct((E, 8, cap), jnp.int32),
                   jax.ShapeDtypeStruct((E, cap, 1), F32)],
        scratch_shapes=[pltpu.VMEM((E, nb, LANES), F32), pltpu.VMEM((E, nb, LANES), F32)],
        compiler_params=pltpu.CompilerParams(vmem_limit_bytes=VMEM_LIMIT),
        name="ec_topk",
    )(aff_t.reshape(E, nb, LANES))
    return idx[:, 0, :], gates


FFN_ROWS = 1024
FFN_FC = 256
FFN_DC = 512


def _ffn_body(idx_ref, hn_hbm, yin_hbm, gate_ref, wg_ref, wu_ref, wd_ref, y_hbm,
              xrows_ref, yrows_ref, xe_ref, acc_ref, sem, *, cap, rows):
    del yin_hbm
    e, r, f = pl.program_id(0), pl.program_id(1), pl.program_id(2)
    nr, nf = pl.num_programs(1), pl.num_programs(2)
    block = e * nr + r
    base = block * rows
    last_f = f == nf - 1

    def hbm_token(ref, slot):
        return ref.at[pl.ds(idx_ref[slot] * TOK_PITCH, TOK_ROWS)]

    def vmem_token(ref, s):
        return ref.at[pl.ds(s * TOK_PITCH, TOK_ROWS)]

    def x_gather(first_slot, s):
        return pltpu.make_async_copy(hbm_token(hn_hbm, first_slot + s), vmem_token(xrows_ref, s), sem.at[0])

    def y_gather(s):
        return pltpu.make_async_copy(hbm_token(y_hbm, base + s), vmem_token(yrows_ref, s), sem.at[1])

    def scatter(s):
        return pltpu.make_async_copy(vmem_token(yrows_ref, s), hbm_token(y_hbm, base + s), sem.at[2])

    def for_rows(fn):
        def body(s, c):
            fn(s)
            return c
        lax.fori_loop(0, rows, body, 0, unroll=8)

    @pl.when((f == 0) & (block == 0))
    def _():
        for_rows(lambda s: x_gather(base, s).start())

    @pl.when(f == 0)
    def _():
        for_rows(lambda s: x_gather(base, s).wait())
        for a, col in enumerate(_load_token_major(xrows_ref, rows)):
            xe_ref[:, a * LANES:(a + 1) * LANES] = col.astype(BF16)

    @pl.when((f == 1) & (block + 1 < pl.num_programs(0) * nr))
    def _():
        for_rows(lambda s: x_gather(base + rows, s).start())

    @pl.when(last_f)
    def _():
        for_rows(lambda s: y_gather(s).start())

    x = xe_ref[...]
    g = jnp.dot(x, wg_ref[...].astype(BF16), preferred_element_type=F32)
    u = jnp.dot(x, wu_ref[...].astype(BF16), preferred_element_type=F32)
    hid = (jax.nn.silu(g) * u).astype(BF16)
    wd = wd_ref[...].astype(BF16)

    def down(accumulate):
        for c in range(D_MODEL // FFN_DC):
            cs = slice(c * FFN_DC, (c + 1) * FFN_DC)
            part = jnp.dot(hid, wd[:, cs], preferred_element_type=F32)
            acc_ref[:, cs] = acc_ref[:, cs] + part if accumulate else part

    pl.when(f == 0)(lambda: down(False))
    pl.when(f != 0)(lambda: down(True))

    @pl.when(last_f)
    def _():
        for_rows(lambda s: y_gather(s).wait())
        gate = gate_ref[...]
        for a in range(TOK_ROWS):
            group = pl.ds(a, rows, stride=TOK_PITCH)
            yrows_ref[group, :] = yrows_ref[group, :] + acc_ref[:, a * LANES:(a + 1) * LANES] * gate
        for_rows(lambda s: scatter(s).start())
        for_rows(lambda s: scatter(s).wait())


def _ffn_call(idx, gates, hn, h, w_gate, w_up, w_down):
    E, cap = idx.shape
    D = D_MODEL
    F = w_gate.shape[2]
    rows = min(FFN_ROWS, cap)
    tokbuf = pltpu.VMEM((rows * TOK_PITCH, LANES), F32)
    grid_spec = pltpu.PrefetchScalarGridSpec(
        num_scalar_prefetch=1,
        grid=(E, cap // rows, F // FFN_FC),
        in_specs=[
            pl.BlockSpec(memory_space=pl.ANY),
            pl.BlockSpec(memory_space=pl.ANY),
            pl.BlockSpec((None, rows, 1), lambda e, r, f, idx: (e, r, 0)),
            pl.BlockSpec((None, D, FFN_FC), lambda e, r, f, idx: (e, 0, f)),
            pl.BlockSpec((None, D, FFN_FC), lambda e, r, f, idx: (e, 0, f)),
            pl.BlockSpec((None, FFN_FC, D), lambda e, r, f, idx: (e, f, 0)),
        ],
        out_specs=pl.BlockSpec(memory_space=pl.ANY),
        scratch_shapes=[tokbuf, tokbuf, pltpu.VMEM((rows, D), BF16), pltpu.VMEM((rows, D), F32),
                        pltpu.SemaphoreType.DMA((3,))],
    )
    assert F // FFN_FC >= 2, "the next block's row gather is issued from hidden chunk 1"
    return pl.pallas_call(
        functools.partial(_ffn_body, cap=cap, rows=rows),
        grid_spec=grid_spec,
        out_shape=jax.ShapeDtypeStruct(h.shape, F32),
        input_output_aliases={2: 0},
        compiler_params=_cparams(("arbitrary", "arbitrary", "arbitrary")),
        name="ec_ffn",
    )(idx.reshape(-1), hn, h, gates, w_gate, w_up, w_down)


def _row_major_body(y_ref, o_ref):
    for a, col in enumerate(_load_token_major(y_ref, o_ref.shape[0])):
        o_ref[:, a * LANES:(a + 1) * LANES] = col


def _row_major_call(y, B, S, tm):
    nblk = S // tm
    return pl.pallas_call(
        _row_major_body,
        grid=(B, nblk),
        in_specs=[pl.BlockSpec((tm * TOK_PITCH, LANES), lambda b, i: (b * nblk + i, 0))],
        out_specs=pl.BlockSpec((None, tm, D_MODEL), lambda b, i: (b, i, 0)),
        out_shape=jax.ShapeDtypeStruct((B, S, D_MODEL), F32),
        compiler_params=_cparams(("parallel", "arbitrary")),
        name="row_major",
    )(y)


def _rope_tables(S):
    t = np.arange(S)
    row = (t // GRID_W).astype(np.float32)
    col = (t % GRID_W).astype(np.float32)
    nf = ROPE_DIM // 4
    inv = (1.0 / (ROPE_BASE ** (jnp.arange(nf, dtype=F32) / nf)))
    ar = jnp.asarray(row)[:, None] * inv
    ac = jnp.asarray(col)[:, None] * inv
    cr, sr, cc, sc = jnp.cos(ar), jnp.sin(ar), jnp.cos(ac), jnp.sin(ac)
    z = jnp.zeros((S, LANES - ROPE_DIM), F32)
    cos = jnp.concatenate([cr, cr, cc, cc, z], axis=1)
    sin = jnp.concatenate([-sr, sr, -sc, sc, z], axis=1)
    return cos, sin


def _pad_lanes(v, fill=0.0):
    return jnp.concatenate([v, jnp.full((LANES - v.shape[0],), fill, v.dtype)])[None, :]


def _prepare(S, norm1_g, w_in, na_q_norm, na_k_norm, na_rpb, q_a_norm, w_q_up, kv_a_norm, w_kv_up,
             mla_qn_norm, mla_qr_norm, mla_kn_norm, mla_kr_norm, na_out_norm, mla_out_norm, w_out,
             norm2_g, w_router):
    p = {}
    p['g1'] = norm1_g[None, :]
    p['win'] = jnp.concatenate(
        [w_in, jnp.zeros((D_MODEL, ROPE_PAD - ROPE_DIM), w_in.dtype)], axis=1).astype(BF16)
    wq = w_q_up.reshape(Q_LORA, MLA_HEADS, QK_DIM)
    wq_nope = wq[:, :, :NOPE_DIM].reshape(Q_LORA, MLA_HEADS * NOPE_DIM)
    wq_pe = jnp.pad(wq[:, :, NOPE_DIM:], ((0, 0), (0, 0), (0, LANES - ROPE_DIM)))
    p['wq'] = jnp.concatenate([wq_nope, wq_pe.reshape(Q_LORA, MLA_HEADS * LANES)], axis=1).astype(BF16)
    wkv = w_kv_up.reshape(KV_LORA, MLA_HEADS, NOPE_DIM + V_DIM)
    p['wkv'] = jnp.concatenate([wkv[:, :, :NOPE_DIM].reshape(KV_LORA, -1),
                                wkv[:, :, NOPE_DIM:].reshape(KV_LORA, -1)], axis=1).astype(BF16)
    p['gq'] = (na_q_norm * (NA_HEAD_DIM ** -0.5 * LOG2E))[None, :]
    p['gk'] = na_k_norm[None, :]
    p['gqa'] = q_a_norm[None, :]
    p['gkva'] = kv_a_norm[None, :]
    p['gkr'] = _pad_lanes(mla_kr_norm)
    p['gqn'] = (mla_qn_norm * (QK_DIM ** -0.5 * LOG2E))[None, :]
    p['gqr'] = _pad_lanes(mla_qr_norm * (QK_DIM ** -0.5 * LOG2E))
    p['gkn'] = mla_kn_norm[None, :]
    p['cos'], p['sin'] = _rope_tables(S)
    p['bias'] = _na_bias_table(na_rpb * LOG2E)
    p['gna'] = na_out_norm[None, :]
    p['gmla'] = mla_out_norm[None, :]
    p['wout'] = w_out.astype(BF16)
    p['g2'] = norm2_g[None, :]
    p['wr'] = jnp.pad(w_router, ((0, 0), (0, LANES - N_EXPERTS))).astype(BF16)
    return p


def _attention_block(x, p):
    qna, kna, vna, qn, qpe, kn, vm, kpe = _proj_call(x, p, tm=512)
    out_na = _na_call(qna, kna, vna, p['bias'])
    out_mla = _mla_call(qn, qpe, kn, kpe, vm, tq=x.shape[1])
    return _out_call(out_na, out_mla, x, p, tm=512)


def _layer(x, p, w_gate, w_up, w_down):
    B, S, D = x.shape
    h, hn, aff_t = _attention_block(x, p)
    idx, gates = _topk_call(aff_t)
    y = _ffn_call(idx, gates, hn, h, w_gate, w_up, w_down)
    return _row_major_call(y, B, S, tm=512)


def kernel(x_prompt, x_sample, norm1_g, w_in, na_q_norm, na_k_norm, na_rpb, q_a_norm, w_q_up, kv_a_norm, w_kv_up, mla_qn_norm, mla_qr_norm, mla_kn_norm, mla_kr_norm, na_out_norm, mla_out_norm, w_out, norm2_g, w_router, w_gate, w_up, w_down):
    hp, hs = x_prompt, x_sample
    for l in range(norm1_g.shape[0]):
        p = _prepare(hp.shape[1], norm1_g[l], w_in[l], na_q_norm[l], na_k_norm[l], na_rpb[l],
                     q_a_norm[l], w_q_up[l], kv_a_norm[l], w_kv_up[l], mla_qn_norm[l],
                     mla_qr_norm[l], mla_kn_norm[l], mla_kr_norm[l], na_out_norm[l],
                     mla_out_norm[l], w_out[l], norm2_g[l], w_router[l])
        hp = _layer(hp, p, w_gate[l], w_up[l], w_down[l])
        hs = _layer(hs, p, w_gate[l], w_up[l], w_down[l])
    return (hp, hs)
```

```python
import functools

import jax
import jax.numpy as jnp
import numpy as np
from jax import lax
from jax.experimental import pallas as pl
from jax.experimental.pallas import tpu as pltpu

D_MODEL = 2048
GRID_W = 64
NA_HEADS = 8
NA_HEAD_DIM = 128
NA_WIN_H = 8
NA_WIN_W = 16
MLA_HEADS = 8
Q_LORA = 512
KV_LORA = 512
NOPE_DIM = 128
ROPE_DIM = 64
V_DIM = 128
ROPE_BASE = 10000.0
N_EXPERTS = 16
EC_CAPACITY = 2
EXPERT_FF = 1024
EPS = 1e-6

NA_WIDTH = NA_HEADS * NA_HEAD_DIM
MLA_WIDTH = MLA_HEADS * V_DIM
QK_DIM = NOPE_DIM + ROPE_DIM
LANES = 128
ROPE_PAD = LANES
P_PAD = 3 * NA_WIDTH + Q_LORA + KV_LORA + ROPE_PAD
NEG = -1e30
LOG2E = 1.4426950408889634

BF16 = jnp.bfloat16
F32 = jnp.float32

VMEM_LIMIT = 56 * 1024 * 1024


def _cparams(sem):
    return pltpu.CompilerParams(dimension_semantics=sem, vmem_limit_bytes=VMEM_LIMIT)


def _resident(shape):
    nd = len(shape)
    return pl.BlockSpec(shape, lambda *_: (0,) * nd, pipeline_mode=pl.Buffered(1))


def _rms(a, width):
    return lax.rsqrt(jnp.sum(a * a, axis=-1, keepdims=True) * (1.0 / width) + EPS)


def _rope(a, cos, sin):
    lane = lax.broadcasted_iota(jnp.int32, a.shape, 1)
    up = pltpu.roll(a, LANES - 16, axis=1)
    dn = pltpu.roll(a, 16, axis=1)
    partner = jnp.where((lane % 32) < 16, up, dn)
    return a * cos + partner * sin


def _proj_body(x_ref, g1_ref, win_ref, wq_ref, wkv_ref, gq_ref, gk_ref, gqa_ref, gkva_ref,
               gkr_ref, gqn_ref, gqr_ref, gkn_ref, cos_ref, sin_ref,
               qna_ref, kna_ref, vna_ref, qn_ref, qpe_ref, kn_ref, vm_ref, kpe_ref):
    x = x_ref[...]
    xn = (x * _rms(x, D_MODEL) * g1_ref[...]).astype(BF16)
    cos = cos_ref[...]
    sin = sin_ref[...]

    def proj(lo, hi):
        return jnp.dot(xn, win_ref[:, lo:hi], preferred_element_type=F32)

    def heads_to(ref, a, g, width, base, rope=False):
        for h in range(a.shape[1] // LANES):
            ah = a[:, h * LANES:(h + 1) * LANES]
            y = ah * _rms(ah, width) * g
            if rope:
                y = _rope(y, cos, sin)
            ref[:, base + h * LANES: base + (h + 1) * LANES] = y.astype(BF16)

    half = NA_WIDTH // 2
    for c in range(2):
        heads_to(qna_ref, proj(c * half, (c + 1) * half), gq_ref[...], NA_HEAD_DIM, c * half)
    for c in range(2):
        heads_to(kna_ref, proj(NA_WIDTH + c * half, NA_WIDTH + (c + 1) * half), gk_ref[...],
                 NA_HEAD_DIM, c * half)
    for c in range(2):
        vna_ref[:, c * half:(c + 1) * half] = proj(2 * NA_WIDTH + c * half,
                                                   2 * NA_WIDTH + (c + 1) * half).astype(BF16)

    o = 3 * NA_WIDTH
    ql = proj(o, o + Q_LORA)
    qln = (ql * _rms(ql, Q_LORA) * gqa_ref[...]).astype(BF16)
    kvl = proj(o + Q_LORA, o + Q_LORA + KV_LORA)
    kvln = (kvl * _rms(kvl, KV_LORA) * gkva_ref[...]).astype(BF16)
    kp = proj(o + Q_LORA + KV_LORA, P_PAD)
    heads_to(kpe_ref, kp, gkr_ref[...], ROPE_DIM, 0, rope=True)

    nh = MLA_HEADS * LANES
    for c in range(2):
        a = jnp.dot(qln, wq_ref[:, c * half:(c + 1) * half], preferred_element_type=F32)
        heads_to(qn_ref, a, gqn_ref[...], NOPE_DIM, c * half)
    for c in range(2):
        a = jnp.dot(qln, wq_ref[:, nh + c * half: nh + (c + 1) * half], preferred_element_type=F32)
        heads_to(qpe_ref, a, gqr_ref[...], ROPE_DIM, c * half, rope=True)
    for c in range(2):
        a = jnp.dot(kvln, wkv_ref[:, c * half:(c + 1) * half], preferred_element_type=F32)
        heads_to(kn_ref, a, gkn_ref[...], NOPE_DIM, c * half)
    for c in range(2):
        vm_ref[:, c * half:(c + 1) * half] = jnp.dot(
            kvln, wkv_ref[:, nh + c * half: nh + (c + 1) * half],
            preferred_element_type=F32).astype(BF16)


def _proj_call(x, p, tm):
    B, S, D = x.shape
    grid = (B, S // tm)
    row = lambda w: pl.BlockSpec((None, tm, w), lambda b, i: (b, i, 0))
    tab = pl.BlockSpec((tm, LANES), lambda b, i: (i, 0))
    vec = lambda w: _resident((1, w))
    out_w = (NA_WIDTH, NA_WIDTH, NA_WIDTH, MLA_HEADS * LANES, MLA_HEADS * LANES,
             MLA_HEADS * LANES, MLA_WIDTH, LANES)
    return pl.pallas_call(
        _proj_body,
        grid=grid,
        in_specs=[row(D), vec(D), _resident((D, P_PAD)), _resident(p['wq'].shape),
                  _resident(p['wkv'].shape), vec(LANES), vec(LANES), vec(Q_LORA), vec(KV_LORA),
                  vec(LANES), vec(LANES), vec(LANES), vec(LANES), tab, tab],
        out_specs=[row(w) for w in out_w],
        out_shape=[jax.ShapeDtypeStruct((B, S, w), BF16) for w in out_w],
        compiler_params=_cparams(("parallel", "arbitrary")),
        name="in_proj",
    )(x, p['g1'], p['win'], p['wq'], p['wkv'], p['gq'], p['gk'], p['gqa'], p['gkva'],
      p['gkr'], p['gqn'], p['gqr'], p['gkn'], p['cos'], p['sin'])


NA_QROWS = 4
NA_KROWS = NA_QROWS + NA_WIN_H


def _na_window_start(i, rows):
    return int(np.clip(NA_QROWS * i - NA_WIN_H // 2, 0, rows - NA_KROWS))


def _na_build_bias(tab_ref, bias_ref, rows):
    kh = min(NA_WIN_H, rows)
    shape = (GRID_W, LANES)
    c = lax.broadcasted_iota(jnp.int32, shape, 0)
    l = lax.broadcasted_iota(jnp.int32, shape, 1)
    cs = jnp.clip(c - NA_WIN_W // 2, 0, GRID_W - NA_WIN_W)
    in_lo = (l >= cs) & (l < cs + NA_WIN_W)
    in_hi = (l - GRID_W >= cs) & (l - GRID_W < cs + NA_WIN_W)
    neg = jnp.full(shape, NEG, F32)
    lo, hi = [], []
    for d in range(2 * NA_WIN_H - 1):
        base = jnp.broadcast_to(tab_ref[d:d + 1, :], shape)
        lo.append(jnp.where(in_lo, pltpu.roll(base, 0, 1, stride=1, stride_axis=0), NEG))
        hi.append(jnp.where(in_hi, pltpu.roll(base, GRID_W, 1, stride=1, stride_axis=0), NEG))
    for i in range(rows // NA_QROWS):
        ws = _na_window_start(i, rows)
        for qr in range(NA_QROWS):
            r = NA_QROWS * i + qr
            rs = int(np.clip(r - kh // 2, 0, rows - kh))
            for m in range(NA_KROWS // 2):
                k0 = ws + 2 * m
                a = lo[k0 - r + NA_WIN_H - 1] if rs <= k0 < rs + kh else neg
                b = hi[k0 + 1 - r + NA_WIN_H - 1] if rs <= k0 + 1 < rs + kh else neg
                bias_ref[i, qr * GRID_W:(qr + 1) * GRID_W, m * LANES:(m + 1) * LANES] = (
                    jnp.where(l < GRID_W, a, b))


def _na_body(q_ref, k_ref, v_ref, tab_ref, o_ref, bias_ref, vone_ref, *, rows):
    tq = NA_QROWS * GRID_W
    tk = NA_KROWS * GRID_W

    @pl.when(pl.program_id(1) == 0)
    def _():
        _na_build_bias(tab_ref, bias_ref, rows)

    vone_ref[:, :LANES] = v_ref[...]
    vone_ref[:, LANES:] = jnp.ones(v_ref.shape, BF16)

    for i in range(rows // NA_QROWS):
        ws = _na_window_start(i, rows) * GRID_W
        k = k_ref[ws:ws + tk, :]
        v_ones = vone_ref[ws:ws + tk, :]
        for c in range(tq // ATTN_SUB):
            lo = i * tq + c * ATTN_SUB
            s = _dot_t(q_ref[lo:lo + ATTN_SUB, :], k) + bias_ref[i, c * ATTN_SUB:(c + 1) * ATTN_SUB, :]
            o_ref[lo:lo + ATTN_SUB, :] = _softmax_pv(s, v_ones).astype(BF16)


def _na_bias_table(rpb):
    H, nd, nc = rpb.shape
    half = nc // 2
    fill = jnp.full((H, nd, LANES - nc), NEG, rpb.dtype)
    tab = jnp.concatenate([rpb[:, :, half:], fill, rpb[:, :, :half]], axis=2)
    return jnp.concatenate([tab, jnp.full((H, 16 - nd, LANES), NEG, rpb.dtype)], axis=1)


def _na_call(q, k, v, tab):
    B, S, _ = q.shape
    rows = S // GRID_W
    hb = lambda: pl.BlockSpec((None, S, LANES), lambda h, b: (b, 0, h))
    return pl.pallas_call(
        functools.partial(_na_body, rows=rows),
        grid=(NA_HEADS, B),
        in_specs=[hb(), hb(), hb(),
                  pl.BlockSpec((None,) + tab.shape[1:], lambda h, b: (h, 0, 0))],
        out_specs=hb(),
        out_shape=jax.ShapeDtypeStruct((B, S, NA_WIDTH), BF16),
        scratch_shapes=[pltpu.VMEM((rows // NA_QROWS, NA_QROWS * GRID_W, NA_KROWS * GRID_W), F32),
                        pltpu.VMEM((S, 2 * LANES), BF16)],
        compiler_params=_cparams(("arbitrary", "arbitrary")),
        name="na_attn",
    )(q, k, v, tab)


ATTN_SUB = 256


def _softmax_pv(s, v_ones):
    m = jnp.max(s, axis=-1, keepdims=True)
    e = jnp.exp2(s - m).astype(BF16)
    ol = jnp.dot(e, v_ones, preferred_element_type=F32)
    return ol[:, :LANES] / ol[:, LANES:]


def _mla_body(qn_ref, qpe_ref, kn_ref, kpe_ref, v_ref, o_ref, kcat_ref, vone_ref):
    @pl.when(pl.program_id(2) == 0)
    def _():
        kcat_ref[:, :LANES] = kn_ref[...]
        kcat_ref[:, LANES:] = kpe_ref[...]
        vone_ref[:, :LANES] = v_ref[...]
        vone_ref[:, LANES:] = jnp.ones(v_ref.shape, BF16)

    for c in range(qn_ref.shape[0] // ATTN_SUB):
        rows = slice(c * ATTN_SUB, (c + 1) * ATTN_SUB)
        q = jnp.concatenate([qn_ref[rows, :], qpe_ref[rows, :]], axis=1)
        s = _dot_t(q, kcat_ref[...])
        o_ref[rows, :] = _softmax_pv(s, vone_ref[...]).astype(BF16)


def _mla_call(qn, qpe, kn, kpe, v, tq):
    B, S, _ = qn.shape
    qb = lambda: pl.BlockSpec((None, tq, LANES), lambda b, h, i: (b, i, h))
    kb = lambda: pl.BlockSpec((None, S, LANES), lambda b, h, i: (b, 0, h))
    return pl.pallas_call(
        _mla_body,
        grid=(B, MLA_HEADS, S // tq),
        in_specs=[qb(), qb(), kb(), pl.BlockSpec((None, S, LANES), lambda b, h, i: (b, 0, 0)), kb()],
        out_specs=qb(),
        out_shape=jax.ShapeDtypeStruct((B, S, MLA_WIDTH), BF16),
        scratch_shapes=[pltpu.VMEM((S, 2 * LANES), BF16), pltpu.VMEM((S, 2 * LANES), BF16)],
        compiler_params=_cparams(("parallel", "arbitrary", "arbitrary")),
        name="mla_attn",
    )(qn, qpe, kn, kpe, v)


TOK_ROWS = D_MODEL // LANES
TOK_PITCH = TOK_ROWS + 1


def _store_token_major(ref, val):
    t = val.shape[0]
    for a in range(TOK_ROWS):
        ref[pl.ds(a, t, stride=TOK_PITCH), :] = val[:, a * LANES:(a + 1) * LANES]
    ref[pl.ds(TOK_ROWS, t, stride=TOK_PITCH), :] = jnp.zeros((t, LANES), val.dtype)


def _load_token_major(ref, t):
    return [ref[pl.ds(a, t, stride=TOK_PITCH), :] for a in range(TOK_ROWS)]


def _out_body(na_ref, mla_ref, x_ref, gna_ref, gmla_ref, wout_ref, g2_ref, wr_ref,
              h_ref, hn_ref, aff_ref):
    a = na_ref[...].astype(F32)
    an = (a * _rms(a, NA_WIDTH) * gna_ref[...]).astype(BF16)
    b = mla_ref[...].astype(F32)
    bn = (b * _rms(b, MLA_WIDTH) * gmla_ref[...]).astype(BF16)
    h = x_ref[...]
    h = h + jnp.dot(an, wout_ref[:NA_WIDTH, :], preferred_element_type=F32)
    h = h + jnp.dot(bn, wout_ref[NA_WIDTH:, :], preferred_element_type=F32)
    hn = h * _rms(h, D_MODEL) * g2_ref[...]
    _store_token_major(h_ref, h)
    _store_token_major(hn_ref, hn)
    logits = jnp.dot(hn.astype(BF16), wr_ref[...], preferred_element_type=F32)
    lane = lax.broadcasted_iota(jnp.int32, logits.shape, 1)
    logits = jnp.where(lane < N_EXPERTS, logits, NEG)
    m = jnp.max(logits, axis=-1, keepdims=True)
    e = jnp.exp(logits - m)
    aff = e / jnp.sum(e, axis=-1, keepdims=True)
    aff_ref[...] = aff.T[:N_EXPERTS, :]


def _out_call(na, mla, x, p, tm):
    B, S, D = x.shape
    row = lambda w: pl.BlockSpec((None, tm, w), lambda b, i: (b, i, 0))
    nblk = S // tm
    tokmaj = lambda: pl.BlockSpec((tm * TOK_PITCH, LANES), lambda b, i: (b * nblk + i, 0))
    return pl.pallas_call(
        _out_body,
        grid=(B, nblk),
        in_specs=[row(NA_WIDTH), row(MLA_WIDTH), row(D), _resident((1, NA_WIDTH)),
                  _resident((1, MLA_WIDTH)), _resident((NA_WIDTH + MLA_WIDTH, D)),
                  _resident((1, D)), _resident((D, LANES))],
        out_specs=[tokmaj(), tokmaj(), pl.BlockSpec((N_EXPERTS, tm), lambda b, i: (0, b * nblk + i))],
        out_shape=[jax.ShapeDtypeStruct((B * S * TOK_PITCH, LANES), F32),
                   jax.ShapeDtypeStruct((B * S * TOK_PITCH, LANES), F32),
                   jax.ShapeDtypeStruct((N_EXPERTS, B * S), F32)],
        compiler_params=_cparams(("parallel", "arbitrary")),
        name="out_proj",
    )(na, mla, x, p['gna'], p['gmla'], p['wout'], p['g2'], p['wr'])


def _tri(n, strict, upper):
    r = lax.broadcasted_iota(jnp.int32, (n, n), 0)
    c = lax.broadcasted_iota(jnp.int32, (n, n), 1)
    if upper:
        m = (r < c) if strict else (r <= c)
    else:
        m = (c < r) if strict else (c <= r)
    return jnp.where(m, 1.0, 0.0).astype(BF16)


def _dot_t(a, b):
    return lax.dot_general(a, b, (((1,), (1,)), ((), ())), preferred_element_type=F32)


def _topk_body(aff_ref, idx_ref, gate_ref, sel_ref, cum_ref, *, nb, cap):
    E = N_EXPERTS
    aff = aff_ref[...]
    bits = pltpu.bitcast(aff, jnp.int32)

    def count(mask):
        c = jnp.sum(jnp.where(mask, 1.0, 0.0), axis=1, keepdims=True)
        return jnp.sum(c, axis=2, keepdims=True)

    def search(b, t):
        cand = t | jnp.left_shift(jnp.int32(1), 30 - b)
        return jnp.where(count(bits >= cand) >= cap, cand, t)

    t = lax.fori_loop(0, 31, search, jnp.zeros((E, 1, 1), jnp.int32))

    upper_incl = _tri(LANES, strict=False, upper=True)
    lower_strict = jnp.broadcast_to(_tri(nb, strict=True, upper=False)[None], (E, nb, nb))

    def cumsum_incl(m):
        mb = m.astype(BF16)
        within = jnp.dot(mb.reshape(E * nb, LANES), upper_incl,
                         preferred_element_type=F32).reshape(E, nb, LANES)
        before = lax.dot_general(lower_strict, mb, (((2,), (1,)), ((0,), (0,))),
                                 preferred_element_type=F32)
        return within + jnp.sum(before, axis=2, keepdims=True)

    gt = bits > t
    eq = jnp.where(bits == t, 1.0, 0.0)
    need = cap - count(gt)
    eq_before = cumsum_incl(eq) - eq
    sel = jnp.where(gt | ((eq > 0.5) & (eq_before < need)), 1.0, 0.0)
    sel_ref[...] = sel
    cum_ref[...] = cumsum_incl(sel)

    ones_rows = jnp.ones((8, LANES), BF16)
    lane0 = jnp.where(lax.broadcasted_iota(jnp.int32, (8, LANES), 1) == 0, 1.0, 0.0).astype(BF16)
    ones_nb = jnp.ones((nb, LANES), BF16)
    ones_sq = jnp.ones((LANES, LANES), BF16)
    upper_strict_nb = _tri(nb, strict=True, upper=True)
    slot_nb = lax.broadcasted_iota(jnp.int32, (cap, nb), 0).astype(F32)
    slot = lax.broadcasted_iota(jnp.int32, (cap, LANES), 0).astype(F32)
    lane = lax.broadcasted_iota(jnp.int32, (cap, LANES), 1).astype(F32)

    def compact(e, carry):
        m = sel_ref[e].astype(BF16)
        c = cum_ref[e]
        row_tot = _dot_t(ones_rows, m)
        row_off = jnp.dot(row_tot.astype(BF16), upper_strict_nb, preferred_element_type=F32)
        row_end = row_off + row_tot
        off_b = row_off[0:1, :]
        end_b = row_end[0:1, :]
        onehot = jnp.where((off_b <= slot_nb) & (slot_nb < end_b), 1.0, 0.0).astype(BF16)
        rows_before = jnp.where(end_b <= slot_nb, 1.0, 0.0).astype(BF16)
        row_idx = jnp.dot(rows_before, ones_nb, preferred_element_type=F32)
        c_hi = jnp.floor(c * (1.0 / LANES))
        c_lo = c - c_hi * LANES
        c_row = (jnp.dot(onehot, c_hi.astype(BF16), preferred_element_type=F32) * LANES
                 + jnp.dot(onehot, c_lo.astype(BF16), preferred_element_type=F32))
        before = jnp.where(c_row <= slot, 1.0, 0.0).astype(BF16)
        lane_idx = jnp.dot(before, ones_sq, preferred_element_type=F32)
        tok = (_dot_t(lane0, row_idx.astype(BF16)) * LANES
               + _dot_t(lane0, lane_idx.astype(BF16)))
        idx_ref[e] = tok.astype(jnp.int32)
        a = aff_ref[e]
        a1 = a.astype(BF16)
        r1 = a - a1.astype(F32)
        a2 = r1.astype(BF16)
        a3 = (r1 - a2.astype(F32)).astype(BF16)
        a_row = (jnp.dot(onehot, a1, preferred_element_type=F32)
                 + jnp.dot(onehot, a2, preferred_element_type=F32)
                 + jnp.dot(onehot, a3, preferred_element_type=F32))
        gate_ref[e] = jnp.sum(jnp.where(lane == lane_idx, a_row, 0.0), axis=1, keepdims=True)
        return carry

    lax.fori_loop(0, E, compact, 0)


def _topk_call(aff_t):
    E, N = aff_t.shape
    nb = N // LANES
    cap = EC_CAPACITY * N // N_EXPERTS
    idx, gates = pl.pallas_call(
        functools.partial(_topk_body, nb=nb, cap=cap),
        out_shape=[jax.ShapeDtypeStruct((E, 8, cap), jnp.int32),
                   jax.ShapeDtypeStruct((E, cap, 1), F32)],
        scratch_shapes=[pltpu.VMEM((E, nb, LANES), F32), pltpu.VMEM((E, nb, LANES), F32)],
        compiler_params=pltpu.CompilerParams(vmem_limit_bytes=VMEM_LIMIT),
        name="ec_topk",
    )(aff_t.reshape(E, nb, LANES))
    return idx[:, 0, :], gates


FFN_ROWS = 1024
FFN_FC = 256
FFN_DC = 512
FFN_UNROLL = 8


def _ffn_body(idx_ref, hn_hbm, yin_hbm, gate_ref, wg_ref, wu_ref, wd_ref, y_hbm,
              xrows_ref, yrows_ref, xe_ref, acc_ref, sem, *, cap, rows):
    del yin_hbm
    e, r, f = pl.program_id(0), pl.program_id(1), pl.program_id(2)
    nr, nf = pl.num_programs(1), pl.num_programs(2)
    block = e * nr + r
    base = block * rows
    last_f = f == nf - 1

    def hbm_token(ref, slot):
        return ref.at[pl.ds(idx_ref[slot] * TOK_PITCH, TOK_ROWS)]

    def vmem_token(ref, s):
        return ref.at[pl.ds(s * TOK_PITCH, TOK_ROWS)]

    def x_gather(first_slot, s):
        return pltpu.make_async_copy(hbm_token(hn_hbm, first_slot + s), vmem_token(xrows_ref, s), sem.at[0])

    def y_gather(s):
        return pltpu.make_async_copy(hbm_token(y_hbm, base + s), vmem_token(yrows_ref, s), sem.at[1])

    def scatter(s):
        return pltpu.make_async_copy(vmem_token(yrows_ref, s), hbm_token(y_hbm, base + s), sem.at[2])

    def for_rows(fn):
        def body(k, c):
            for j in range(FFN_UNROLL):
                fn(k * FFN_UNROLL + j, j)
            return c
        lax.fori_loop(0, rows // FFN_UNROLL, body, 0)

    @pl.when((f == 0) & (block == 0))
    def _():
        for_rows(lambda s, j: x_gather(base, s).start(priority=1))

    @pl.when(f == 0)
    def _():
        for_rows(lambda s, j: x_gather(base, s).wait())
        for a, col in enumerate(_load_token_major(xrows_ref, rows)):
            xe_ref[:, a * LANES:(a + 1) * LANES] = col.astype(BF16)

    @pl.when((f == 1) & (block + 1 < pl.num_programs(0) * nr))
    def _():
        for_rows(lambda s, j: x_gather(base + rows, s).start(priority=1))

    @pl.when(last_f)
    def _():
        for_rows(lambda s, j: y_gather(s).start(priority=1))

    x = xe_ref[...]
    g = jnp.dot(x, wg_ref[...].astype(BF16), preferred_element_type=F32)
    u = jnp.dot(x, wu_ref[...].astype(BF16), preferred_element_type=F32)
    hid = (jax.nn.silu(g) * u).astype(BF16)
    wd = wd_ref[...].astype(BF16)

    def down(accumulate):
        for c in range(D_MODEL // FFN_DC):
            cs = slice(c * FFN_DC, (c + 1) * FFN_DC)
            part = jnp.dot(hid, wd[:, cs], preferred_element_type=F32)
            acc_ref[:, cs] = acc_ref[:, cs] + part if accumulate else part

    pl.when(f == 0)(lambda: down(False))
    pl.when(f != 0)(lambda: down(True))

    @pl.when(last_f)
    def _():
        for_rows(lambda s, j: y_gather(s).wait())
        gate = gate_ref[...]
        for a in range(TOK_ROWS):
            group = pl.ds(a, rows, stride=TOK_PITCH)
            yrows_ref[group, :] = yrows_ref[group, :] + acc_ref[:, a * LANES:(a + 1) * LANES] * gate
        for_rows(lambda s, j: scatter(s).start(priority=j % 2))
        for_rows(lambda s, j: scatter(s).wait())


def _ffn_call(idx, gates, hn, h, w_gate, w_up, w_down):
    E, cap = idx.shape
    D = D_MODEL
    F = w_gate.shape[2]
    rows = min(FFN_ROWS, cap)
    tokbuf = pltpu.VMEM((rows * TOK_PITCH, LANES), F32)
    grid_spec = pltpu.PrefetchScalarGridSpec(
        num_scalar_prefetch=1,
        grid=(E, cap // rows, F // FFN_FC),
        in_specs=[
            pl.BlockSpec(memory_space=pl.ANY),
            pl.BlockSpec(memory_space=pl.ANY),
            pl.BlockSpec((None, rows, 1), lambda e, r, f, idx: (e, r, 0)),
            pl.BlockSpec((None, D, FFN_FC), lambda e, r, f, idx: (e, 0, f)),
            pl.BlockSpec((None, D, FFN_FC), lambda e, r, f, idx: (e, 0, f)),
            pl.BlockSpec((None, FFN_FC, D), lambda e, r, f, idx: (e, f, 0)),
        ],
        out_specs=pl.BlockSpec(memory_space=pl.ANY),
        scratch_shapes=[tokbuf, tokbuf, pltpu.VMEM((rows, D), BF16), pltpu.VMEM((rows, D), F32),
                        pltpu.SemaphoreType.DMA((3,))],
    )
    assert F // FFN_FC >= 2, "the next block's row gather is issued from hidden chunk 1"
    return pl.pallas_call(
        functools.partial(_ffn_body, cap=cap, rows=rows),
        grid_spec=grid_spec,
        out_shape=jax.ShapeDtypeStruct(h.shape, F32),
        input_output_aliases={2: 0},
        compiler_params=_cparams(("arbitrary", "arbitrary", "arbitrary")),
        name="ec_ffn",
    )(idx.reshape(-1), hn, h, gates, w_gate, w_up, w_down)


def _row_major_body(y_ref, o_ref):
    for a, col in enumerate(_load_token_major(y_ref, o_ref.shape[0])):
        o_ref[:, a * LANES:(a + 1) * LANES] = col


def _row_major_call(y, B, S, tm):
    nblk = S // tm
    return pl.pallas_call(
        _row_major_body,
        grid=(B, nblk),
        in_specs=[pl.BlockSpec((tm * TOK_PITCH, LANES), lambda b, i: (b * nblk + i, 0))],
        out_specs=pl.BlockSpec((None, tm, D_MODEL), lambda b, i: (b, i, 0)),
        out_shape=jax.ShapeDtypeStruct((B, S, D_MODEL), F32),
        compiler_params=_cparams(("parallel", "arbitrary")),
        name="row_major",
    )(y)


def _rope_tables(S):
    t = np.arange(S)
    row = (t // GRID_W).astype(np.float32)
    col = (t % GRID_W).astype(np.float32)
    nf = ROPE_DIM // 4
    inv = (1.0 / (ROPE_BASE ** (jnp.arange(nf, dtype=F32) / nf)))
    ar = jnp.asarray(row)[:, None] * inv
    ac = jnp.asarray(col)[:, None] * inv
    cr, sr, cc, sc = jnp.cos(ar), jnp.sin(ar), jnp.cos(ac), jnp.sin(ac)
    z = jnp.zeros((S, LANES - ROPE_DIM), F32)
    cos = jnp.concatenate([cr, cr, cc, cc, z], axis=1)
    sin = jnp.concatenate([-sr, sr, -sc, sc, z], axis=1)
    return cos, sin


def _pad_lanes(v, fill=0.0):
    return jnp.concatenate([v, jnp.full((LANES - v.shape[0],), fill, v.dtype)])[None, :]


def _prepare(S, norm1_g, w_in, na_q_norm, na_k_norm, na_rpb, q_a_norm, w_q_up, kv_a_norm, w_kv_up,
             mla_qn_norm, mla_qr_norm, mla_kn_norm, mla_kr_norm, na_out_norm, mla_out_norm, w_out,
             norm2_g, w_router):
    p = {}
    p['g1'] = norm1_g[None, :]
    p['win'] = jnp.concatenate(
        [w_in, jnp.zeros((D_MODEL, ROPE_PAD - ROPE_DIM), w_in.dtype)], axis=1).astype(BF16)
    wq = w_q_up.reshape(Q_LORA, MLA_HEADS, QK_DIM)
    wq_nope = wq[:, :, :NOPE_DIM].reshape(Q_LORA, MLA_HEADS * NOPE_DIM)
    wq_pe = jnp.pad(wq[:, :, NOPE_DIM:], ((0, 0), (0, 0), (0, LANES - ROPE_DIM)))
    p['wq'] = jnp.concatenate([wq_nope, wq_pe.reshape(Q_LORA, MLA_HEADS * LANES)], axis=1).astype(BF16)
    wkv = w_kv_up.reshape(KV_LORA, MLA_HEADS, NOPE_DIM + V_DIM)
    p['wkv'] = jnp.concatenate([wkv[:, :, :NOPE_DIM].reshape(KV_LORA, -1),
                                wkv[:, :, NOPE_DIM:].reshape(KV_LORA, -1)], axis=1).astype(BF16)
    p['gq'] = (na_q_norm * (NA_HEAD_DIM ** -0.5 * LOG2E))[None, :]
    p['gk'] = na_k_norm[None, :]
    p['gqa'] = q_a_norm[None, :]
    p['gkva'] = kv_a_norm[None, :]
    p['gkr'] = _pad_lanes(mla_kr_norm)
    p['gqn'] = (mla_qn_norm * (QK_DIM ** -0.5 * LOG2E))[None, :]
    p['gqr'] = _pad_lanes(mla_qr_norm * (QK_DIM ** -0.5 * LOG2E))
    p['gkn'] = mla_kn_norm[None, :]
    p['cos'], p['sin'] = _rope_tables(S)
    p['bias'] = _na_bias_table(na_rpb * LOG2E)
    p['gna'] = na_out_norm[None, :]
    p['gmla'] = mla_out_norm[None, :]
    p['wout'] = w_out.astype(BF16)
    p['g2'] = norm2_g[None, :]
    p['wr'] = jnp.pad(w_router, ((0, 0), (0, LANES - N_EXPERTS))).astype(BF16)
    return p


def _attention_block(x, p):
    qna, kna, vna, qn, qpe, kn, vm, kpe = _proj_call(x, p, tm=512)
    out_na = _na_call(qna, kna, vna, p['bias'])
    out_mla = _mla_call(qn, qpe, kn, kpe, vm, tq=x.shape[1])
    return _out_call(out_na, out_mla, x, p, tm=512)


def _layer(x, p, w_gate, w_up, w_down):
    B, S, D = x.shape
    h, hn, aff_t = _attention_block(x, p)
    idx, gates = _topk_call(aff_t)
    y = _ffn_call(idx, gates, hn, h, w_gate, w_up, w_down)
    return _row_major_call(y, B, S, tm=512)


def kernel(x_prompt, x_sample, norm1_g, w_in, na_q_norm, na_k_norm, na_rpb, q_a_norm, w_q_up, kv_a_norm, w_kv_up, mla_qn_norm, mla_qr_norm, mla_kn_norm, mla_kr_norm, na_out_norm, mla_out_norm, w_out, norm2_g, w_router, w_gate, w_up, w_down):
    hp, hs = x_prompt, x_sample
    for l in range(norm1_g.shape[0]):
        p = _prepare(hp.shape[1], norm1_g[l], w_in[l], na_q_norm[l], na_k_norm[l], na_rpb[l],
                     q_a_norm[l], w_q_up[l], kv_a_norm[l], w_kv_up[l], mla_qn_norm[l],
                     mla_qr_norm[l], mla_kn_norm[l], mla_kr_norm[l], na_out_norm[l],
                     mla_out_norm[l], w_out[l], norm2_g[l], w_router[l])
        hp = _layer(hp, p, w_gate[l], w_up[l], w_down[l])
        hs = _layer(hs, p, w_gate[l], w_up[l], w_down[l])
    return (hp, hs)
```

```python
import functools

import jax
import jax.numpy as jnp
import numpy as np
from jax import lax
from jax.experimental import pallas as pl
from jax.experimental.pallas import tpu as pltpu

D_MODEL = 2048
GRID_W = 64
NA_HEADS = 8
NA_HEAD_DIM = 128
NA_WIN_H = 8
NA_WIN_W = 16
MLA_HEADS = 8
Q_LORA = 512
KV_LORA = 512
NOPE_DIM = 128
ROPE_DIM = 64
V_DIM = 128
ROPE_BASE = 10000.0
N_EXPERTS = 16
EC_CAPACITY = 2
EXPERT_FF = 1024
EPS = 1e-6

NA_WIDTH = NA_HEADS * NA_HEAD_DIM
MLA_WIDTH = MLA_HEADS * V_DIM
QK_DIM = NOPE_DIM + ROPE_DIM
LANES = 128
ROPE_PAD = LANES
P_PAD = 3 * NA_WIDTH + Q_LORA + KV_LORA + ROPE_PAD
NEG = -1e30
LOG2E = 1.4426950408889634

BF16 = jnp.bfloat16
F32 = jnp.float32

VMEM_LIMIT = 56 * 1024 * 1024


def _cparams(sem):
    return pltpu.CompilerParams(dimension_semantics=sem, vmem_limit_bytes=VMEM_LIMIT)


def _resident(shape):
    nd = len(shape)
    return pl.BlockSpec(shape, lambda *_: (0,) * nd, pipeline_mode=pl.Buffered(1))


def _rms(a, width):
    return lax.rsqrt(jnp.sum(a * a, axis=-1, keepdims=True) * (1.0 / width) + EPS)


def _rope(a, cos, sin):
    lane = lax.broadcasted_iota(jnp.int32, a.shape, 1)
    up = pltpu.roll(a, LANES - 16, axis=1)
    dn = pltpu.roll(a, 16, axis=1)
    partner = jnp.where((lane % 32) < 16, up, dn)
    return a * cos + partner * sin


def _proj_body(x_ref, g1_ref, win_ref, wq_ref, wkv_ref, gq_ref, gk_ref, gqa_ref, gkva_ref,
               gkr_ref, gqn_ref, gqr_ref, gkn_ref, cos_ref, sin_ref,
               qna_ref, kna_ref, vna_ref, qn_ref, qpe_ref, kn_ref, vm_ref, kpe_ref):
    half = NA_WIDTH // 2
    nh = MLA_HEADS * LANES

    def chain(rows):
        x = x_ref[rows, :]
        xn = (x * _rms(x, D_MODEL) * g1_ref[...]).astype(BF16)
        cos = cos_ref[rows, :]
        sin = sin_ref[rows, :]

        def proj(lo, hi):
            return jnp.dot(xn, win_ref[:, lo:hi], preferred_element_type=F32)

        def heads_to(ref, a, g, width, base, rope=False):
            for h in range(a.shape[1] // LANES):
                ah = a[:, h * LANES:(h + 1) * LANES]
                y = ah * _rms(ah, width) * g
                if rope:
                    y = _rope(y, cos, sin)
                ref[rows, base + h * LANES: base + (h + 1) * LANES] = y.astype(BF16)

        for c in range(2):
            heads_to(qna_ref, proj(c * half, (c + 1) * half), gq_ref[...], NA_HEAD_DIM, c * half)
        for c in range(2):
            heads_to(kna_ref, proj(NA_WIDTH + c * half, NA_WIDTH + (c + 1) * half), gk_ref[...],
                     NA_HEAD_DIM, c * half)
        for c in range(2):
            vna_ref[rows, c * half:(c + 1) * half] = proj(2 * NA_WIDTH + c * half,
                                                          2 * NA_WIDTH + (c + 1) * half).astype(BF16)

        o = 3 * NA_WIDTH
        ql = proj(o, o + Q_LORA)
        qln = (ql * _rms(ql, Q_LORA) * gqa_ref[...]).astype(BF16)
        kvl = proj(o + Q_LORA, o + Q_LORA + KV_LORA)
        kvln = (kvl * _rms(kvl, KV_LORA) * gkva_ref[...]).astype(BF16)
        kp = proj(o + Q_LORA + KV_LORA, P_PAD)
        heads_to(kpe_ref, kp, gkr_ref[...], ROPE_DIM, 0, rope=True)

        for c in range(2):
            a = jnp.dot(qln, wq_ref[:, c * half:(c + 1) * half], preferred_element_type=F32)
            heads_to(qn_ref, a, gqn_ref[...], NOPE_DIM, c * half)
        for c in range(2):
            a = jnp.dot(qln, wq_ref[:, nh + c * half: nh + (c + 1) * half], preferred_element_type=F32)
            heads_to(qpe_ref, a, gqr_ref[...], ROPE_DIM, c * half, rope=True)
        for c in range(2):
            a = jnp.dot(kvln, wkv_ref[:, c * half:(c + 1) * half], preferred_element_type=F32)
            heads_to(kn_ref, a, gkn_ref[...], NOPE_DIM, c * half)
        for c in range(2):
            vm_ref[rows, c * half:(c + 1) * half] = jnp.dot(
                kvln, wkv_ref[:, nh + c * half: nh + (c + 1) * half],
                preferred_element_type=F32).astype(BF16)

    for c in range(x_ref.shape[0] // PROJ_SUB):
        chain(slice(c * PROJ_SUB, (c + 1) * PROJ_SUB))


def _proj_call(x, p, tm):
    B, S, D = x.shape
    grid = (B, S // tm)
    row = lambda w: pl.BlockSpec((None, tm, w), lambda b, i: (b, i, 0))
    tab = pl.BlockSpec((tm, LANES), lambda b, i: (i, 0))
    vec = lambda w: _resident((1, w))
    out_w = (NA_WIDTH, NA_WIDTH, NA_WIDTH, MLA_HEADS * LANES, MLA_HEADS * LANES,
             MLA_HEADS * LANES, MLA_WIDTH, LANES)
    return pl.pallas_call(
        _proj_body,
        grid=grid,
        in_specs=[row(D), vec(D), _resident((D, P_PAD)), _resident(p['wq'].shape),
                  _resident(p['wkv'].shape), vec(LANES), vec(LANES), vec(Q_LORA), vec(KV_LORA),
                  vec(LANES), vec(LANES), vec(LANES), vec(LANES), tab, tab],
        out_specs=[row(w) for w in out_w],
        out_shape=[jax.ShapeDtypeStruct((B, S, w), BF16) for w in out_w],
        compiler_params=_cparams(("parallel", "arbitrary")),
        name="in_proj",
    )(x, p['g1'], p['win'], p['wq'], p['wkv'], p['gq'], p['gk'], p['gqa'], p['gkva'],
      p['gkr'], p['gqn'], p['gqr'], p['gkn'], p['cos'], p['sin'])


NA_QROWS = 4
NA_KROWS = NA_QROWS + NA_WIN_H


def _na_window_start(i, rows):
    return int(np.clip(NA_QROWS * i - NA_WIN_H // 2, 0, rows - NA_KROWS))


def _na_build_bias(tab_ref, bias_ref, rows):
    kh = min(NA_WIN_H, rows)
    shape = (GRID_W, LANES)
    c = lax.broadcasted_iota(jnp.int32, shape, 0)
    l = lax.broadcasted_iota(jnp.int32, shape, 1)
    cs = jnp.clip(c - NA_WIN_W // 2, 0, GRID_W - NA_WIN_W)
    in_lo = (l >= cs) & (l < cs + NA_WIN_W)
    in_hi = (l - GRID_W >= cs) & (l - GRID_W < cs + NA_WIN_W)
    neg = jnp.full(shape, NEG, F32)
    lo, hi = [], []
    for d in range(2 * NA_WIN_H - 1):
        base = jnp.broadcast_to(tab_ref[d:d + 1, :], shape)
        lo.append(jnp.where(in_lo, pltpu.roll(base, 0, 1, stride=1, stride_axis=0), NEG))
        hi.append(jnp.where(in_hi, pltpu.roll(base, GRID_W, 1, stride=1, stride_axis=0), NEG))
    for i in range(rows // NA_QROWS):
        ws = _na_window_start(i, rows)
        for qr in range(NA_QROWS):
            r = NA_QROWS * i + qr
            rs = int(np.clip(r - kh // 2, 0, rows - kh))
            for m in range(NA_KROWS // 2):
                k0 = ws + 2 * m
                a = lo[k0 - r + NA_WIN_H - 1] if rs <= k0 < rs + kh else neg
                b = hi[k0 + 1 - r + NA_WIN_H - 1] if rs <= k0 + 1 < rs + kh else neg
                bias_ref[i, qr * GRID_W:(qr + 1) * GRID_W, m * LANES:(m + 1) * LANES] = (
                    jnp.where(l < GRID_W, a, b))


def _na_body(q_ref, k_ref, v_ref, tab_ref, o_ref, bias_ref, vone_ref, *, rows):
    tq = NA_QROWS * GRID_W
    tk = NA_KROWS * GRID_W

    @pl.when(pl.program_id(1) == 0)
    def _():
        _na_build_bias(tab_ref, bias_ref, rows)

    vone_ref[:, :LANES] = v_ref[...]
    vone_ref[:, LANES:] = jnp.ones(v_ref.shape, BF16)

    for i in range(rows // NA_QROWS):
        ws = _na_window_start(i, rows) * GRID_W
        k = k_ref[ws:ws + tk, :]
        v_ones = vone_ref[ws:ws + tk, :]
        for c in range(tq // ATTN_SUB):
            lo = i * tq + c * ATTN_SUB
            s = _dot_t(q_ref[lo:lo + ATTN_SUB, :], k) + bias_ref[i, c * ATTN_SUB:(c + 1) * ATTN_SUB, :]
            o_ref[lo:lo + ATTN_SUB, :] = _softmax_pv(s, v_ones).astype(BF16)


def _na_bias_table(rpb):
    H, nd, nc = rpb.shape
    half = nc // 2
    fill = jnp.full((H, nd, LANES - nc), NEG, rpb.dtype)
    tab = jnp.concatenate([rpb[:, :, half:], fill, rpb[:, :, :half]], axis=2)
    return jnp.concatenate([tab, jnp.full((H, 16 - nd, LANES), NEG, rpb.dtype)], axis=1)


def _na_call(q, k, v, tab):
    B, S, _ = q.shape
    rows = S // GRID_W
    hb = lambda: pl.BlockSpec((None, S, LANES), lambda h, b: (b, 0, h))
    return pl.pallas_call(
        functools.partial(_na_body, rows=rows),
        grid=(NA_HEADS, B),
        in_specs=[hb(), hb(), hb(),
                  pl.BlockSpec((None,) + tab.shape[1:], lambda h, b: (h, 0, 0))],
        out_specs=hb(),
        out_shape=jax.ShapeDtypeStruct((B, S, NA_WIDTH), BF16),
        scratch_shapes=[pltpu.VMEM((rows // NA_QROWS, NA_QROWS * GRID_W, NA_KROWS * GRID_W), F32),
                        pltpu.VMEM((S, 2 * LANES), BF16)],
        compiler_params=_cparams(("arbitrary", "arbitrary")),
        name="na_attn",
    )(q, k, v, tab)


ATTN_SUB = 256


def _softmax_pv(s, v_ones):
    m = jnp.max(s, axis=-1, keepdims=True)
    e = jnp.exp2(s - m).astype(BF16)
    ol = jnp.dot(e, v_ones, preferred_element_type=F32)
    return ol[:, :LANES] / ol[:, LANES:]


def _mla_body(qn_ref, qpe_ref, kn_ref, kpe_ref, v_ref, o_ref, kcat_ref, vone_ref):
    @pl.when(pl.program_id(2) == 0)
    def _():
        kcat_ref[:, :LANES] = kn_ref[...]
        kcat_ref[:, LANES:] = kpe_ref[...]
        vone_ref[:, :LANES] = v_ref[...]
        vone_ref[:, LANES:] = jnp.ones(v_ref.shape, BF16)

    for c in range(qn_ref.shape[0] // ATTN_SUB):
        rows = slice(c * ATTN_SUB, (c + 1) * ATTN_SUB)
        q = jnp.concatenate([qn_ref[rows, :], qpe_ref[rows, :]], axis=1)
        s = _dot_t(q, kcat_ref[...])
        o_ref[rows, :] = _softmax_pv(s, vone_ref[...]).astype(BF16)


def _mla_call(qn, qpe, kn, kpe, v, tq):
    B, S, _ = qn.shape
    qb = lambda: pl.BlockSpec((None, tq, LANES), lambda b, h, i: (b, i, h))
    kb = lambda: pl.BlockSpec((None, S, LANES), lambda b, h, i: (b, 0, h))
    return pl.pallas_call(
        _mla_body,
        grid=(B, MLA_HEADS, S // tq),
        in_specs=[qb(), qb(), kb(), pl.BlockSpec((None, S, LANES), lambda b, h, i: (b, 0, 0)), kb()],
        out_specs=qb(),
        out_shape=jax.ShapeDtypeStruct((B, S, MLA_WIDTH), BF16),
        scratch_shapes=[pltpu.VMEM((S, 2 * LANES), BF16), pltpu.VMEM((S, 2 * LANES), BF16)],
        compiler_params=_cparams(("parallel", "arbitrary", "arbitrary")),
        name="mla_attn",
    )(qn, qpe, kn, kpe, v)


TOK_ROWS = D_MODEL // LANES
TOK_PITCH = TOK_ROWS + 1


PROJ_SUB = 256


def _store_token_major(ref, val, first_token=0):
    t = val.shape[0]
    base = first_token * TOK_PITCH
    for a in range(TOK_ROWS):
        ref[pl.ds(base + a, t, stride=TOK_PITCH), :] = val[:, a * LANES:(a + 1) * LANES]
    ref[pl.ds(base + TOK_ROWS, t, stride=TOK_PITCH), :] = jnp.zeros((t, LANES), val.dtype)


def _load_token_major(ref, t):
    return [ref[pl.ds(a, t, stride=TOK_PITCH), :] for a in range(TOK_ROWS)]


def _out_body(na_ref, mla_ref, x_ref, gna_ref, gmla_ref, wout_ref, g2_ref, wr_ref,
              h_ref, hn_ref, aff_ref):
    for c in range(x_ref.shape[0] // PROJ_SUB):
        rows = slice(c * PROJ_SUB, (c + 1) * PROJ_SUB)
        a = na_ref[rows, :].astype(F32)
        an = (a * _rms(a, NA_WIDTH) * gna_ref[...]).astype(BF16)
        b = mla_ref[rows, :].astype(F32)
        bn = (b * _rms(b, MLA_WIDTH) * gmla_ref[...]).astype(BF16)
        h = x_ref[rows, :]
        h = h + jnp.dot(an, wout_ref[:NA_WIDTH, :], preferred_element_type=F32)
        h = h + jnp.dot(bn, wout_ref[NA_WIDTH:, :], preferred_element_type=F32)
        hn = h * _rms(h, D_MODEL) * g2_ref[...]
        _store_token_major(h_ref, h, c * PROJ_SUB)
        _store_token_major(hn_ref, hn, c * PROJ_SUB)
        logits = jnp.dot(hn.astype(BF16), wr_ref[...], preferred_element_type=F32)
        lane = lax.broadcasted_iota(jnp.int32, logits.shape, 1)
        logits = jnp.where(lane < N_EXPERTS, logits, NEG)
        m = jnp.max(logits, axis=-1, keepdims=True)
        e = jnp.exp(logits - m)
        aff = e / jnp.sum(e, axis=-1, keepdims=True)
        aff_ref[:, rows] = aff.T[:N_EXPERTS, :]


def _out_call(na, mla, x, p, tm):
    B, S, D = x.shape
    row = lambda w: pl.BlockSpec((None, tm, w), lambda b, i: (b, i, 0))
    nblk = S // tm
    tokmaj = lambda: pl.BlockSpec((tm * TOK_PITCH, LANES), lambda b, i: (b * nblk + i, 0))
    return pl.pallas_call(
        _out_body,
        grid=(B, nblk),
        in_specs=[row(NA_WIDTH), row(MLA_WIDTH), row(D), _resident((1, NA_WIDTH)),
                  _resident((1, MLA_WIDTH)), _resident((NA_WIDTH + MLA_WIDTH, D)),
                  _resident((1, D)), _resident((D, LANES))],
        out_specs=[tokmaj(), tokmaj(), pl.BlockSpec((N_EXPERTS, tm), lambda b, i: (0, b * nblk + i))],
        out_shape=[jax.ShapeDtypeStruct((B * S * TOK_PITCH, LANES), F32),
                   jax.ShapeDtypeStruct((B * S * TOK_PITCH, LANES), F32),
                   jax.ShapeDtypeStruct((N_EXPERTS, B * S), F32)],
        compiler_params=_cparams(("parallel", "arbitrary")),
        name="out_proj",
    )(na, mla, x, p['gna'], p['gmla'], p['wout'], p['g2'], p['wr'])


def _tri(n, strict, upper):
    r = lax.broadcasted_iota(jnp.int32, (n, n), 0)
    c = lax.broadcasted_iota(jnp.int32, (n, n), 1)
    if upper:
        m = (r < c) if strict else (r <= c)
    else:
        m = (c < r) if strict else (c <= r)
    return jnp.where(m, 1.0, 0.0).astype(BF16)


def _dot_t(a, b):
    return lax.dot_general(a, b, (((1,), (1,)), ((), ())), preferred_element_type=F32)


def _topk_body(aff_ref, idx_ref, gate_ref, sel_ref, cum_ref, *, nb, cap):
    E = N_EXPERTS
    aff = aff_ref[...]
    bits = pltpu.bitcast(aff, jnp.int32)

    def count(mask):
        c = jnp.sum(jnp.where(mask, 1.0, 0.0), axis=1, keepdims=True)
        return jnp.sum(c, axis=2, keepdims=True)

    def search(b, t):
        cand = t | jnp.left_shift(jnp.int32(1), 30 - b)
        return jnp.where(count(bits >= cand) >= cap, cand, t)

    t = lax.fori_loop(0, 31, search, jnp.zeros((E, 1, 1), jnp.int32))

    upper_incl = _tri(LANES, strict=False, upper=True)
    lower_strict = jnp.broadcast_to(_tri(nb, strict=True, upper=False)[None], (E, nb, nb))

    def cumsum_incl(m):
        mb = m.astype(BF16)
        within = jnp.dot(mb.reshape(E * nb, LANES), upper_incl,
                         preferred_element_type=F32).reshape(E, nb, LANES)
        before = lax.dot_general(lower_strict, mb, (((2,), (1,)), ((0,), (0,))),
                                 preferred_element_type=F32)
        return within + jnp.sum(before, axis=2, keepdims=True)

    gt = bits > t
    eq = jnp.where(bits == t, 1.0, 0.0)
    need = cap - count(gt)
    eq_before = cumsum_incl(eq) - eq
    sel = jnp.where(gt | ((eq > 0.5) & (eq_before < need)), 1.0, 0.0)
    sel_ref[...] = sel
    cum_ref[...] = cumsum_incl(sel)

    ones_rows = jnp.ones((8, LANES), BF16)
    lane0 = jnp.where(lax.broadcasted_iota(jnp.int32, (8, LANES), 1) == 0, 1.0, 0.0).astype(BF16)
    ones_nb = jnp.ones((nb, LANES), BF16)
    ones_sq = jnp.ones((LANES, LANES), BF16)
    upper_strict_nb = _tri(nb, strict=True, upper=True)
    slot_nb = lax.broadcasted_iota(jnp.int32, (cap, nb), 0).astype(F32)
    slot = lax.broadcasted_iota(jnp.int32, (cap, LANES), 0).astype(F32)
    lane = lax.broadcasted_iota(jnp.int32, (cap, LANES), 1).astype(F32)

    def compact(e, carry):
        m = sel_ref[e].astype(BF16)
        c = cum_ref[e]
        row_tot = _dot_t(ones_rows, m)
        row_off = jnp.dot(row_tot.astype(BF16), upper_strict_nb, preferred_element_type=F32)
        row_end = row_off + row_tot
        off_b = row_off[0:1, :]
        end_b = row_end[0:1, :]
        onehot = jnp.where((off_b <= slot_nb) & (slot_nb < end_b), 1.0, 0.0).astype(BF16)
        rows_before = jnp.where(end_b <= slot_nb, 1.0, 0.0).astype(BF16)
        row_idx = jnp.dot(rows_before, ones_nb, preferred_element_type=F32)
        c_hi = jnp.floor(c * (1.0 / LANES))
        c_lo = c - c_hi * LANES
        c_row = (jnp.dot(onehot, c_hi.astype(BF16), preferred_element_type=F32) * LANES
                 + jnp.dot(onehot, c_lo.astype(BF16), preferred_element_type=F32))
        before = jnp.where(c_row <= slot, 1.0, 0.0).astype(BF16)
        lane_idx = jnp.dot(before, ones_sq, preferred_element_type=F32)
        tok = (_dot_t(lane0, row_idx.astype(BF16)) * LANES
               + _dot_t(lane0, lane_idx.astype(BF16)))
        idx_ref[e] = tok.astype(jnp.int32)
        a = aff_ref[e]
        a1 = a.astype(BF16)
        r1 = a - a1.astype(F32)
        a2 = r1.astype(BF16)
        a3 = (r1 - a2.astype(F32)).astype(BF16)
        a_row = (jnp.dot(onehot, a1, preferred_element_type=F32)
                 + jnp.dot(onehot, a2, preferred_element_type=F32)
                 + jnp.dot(onehot, a3, preferred_element_type=F32))
        gate_ref[e] = jnp.sum(jnp.where(lane == lane_idx, a_row, 0.0), axis=1, keepdims=True)
        return carry

    lax.fori_loop(0, E, compact, 0)


def _topk_call(aff_t):
    E, N = aff_t.shape
    nb = N // LANES
    cap = EC_CAPACITY * N // N_EXPERTS
    idx, gates = pl.pallas_call(
        functools.partial(_topk_body, nb=nb, cap=cap),
        out_shape=[jax.ShapeDtypeStruct((E, 8, cap), jnp.int32),
                   jax.ShapeDtypeStruct((E, cap, 1), F32)],
        scratch_shapes=[pltpu.VMEM((E, nb, LANES), F32), pltpu.VMEM((E, nb, LANES), F32)],
        compiler_params=pltpu.CompilerParams(vmem_limit_bytes=VMEM_LIMIT),
        name="ec_topk",
    )(aff_t.reshape(E, nb, LANES))
    return idx[:, 0, :], gates


FFN_ROWS = 1024
FFN_FC = 256
FFN_DC = 512
FFN_UNROLL = 8


def _ffn_body(idx_ref, hn_hbm, yin_hbm, gate_ref, wg_ref, wu_ref, wd_ref, y_hbm,
              xrows_ref, yrows_ref, xe_ref, acc_ref, sem, *, cap, rows):
    del yin_hbm
    e, r, f = pl.program_id(0), pl.program_id(1), pl.program_id(2)
    nr, nf = pl.num_programs(1), pl.num_programs(2)
    block = e * nr + r
    base = block * rows
    last_f = f == nf - 1

    def hbm_token(ref, slot):
        return ref.at[pl.ds(idx_ref[slot] * TOK_PITCH, TOK_ROWS)]

    def vmem_token(ref, s):
        return ref.at[pl.ds(s * TOK_PITCH, TOK_ROWS)]

    def x_gather(first_slot, s):
        return pltpu.make_async_copy(hbm_token(hn_hbm, first_slot + s), vmem_token(xrows_ref, s), sem.at[0])

    def y_gather(s):
        return pltpu.make_async_copy(hbm_token(y_hbm, base + s), vmem_token(yrows_ref, s), sem.at[1])

    def scatter(s):
        return pltpu.make_async_copy(vmem_token(yrows_ref, s), hbm_token(y_hbm, base + s), sem.at[2])

    def for_rows(fn):
        def body(k, c):
            for j in range(FFN_UNROLL):
                fn(k * FFN_UNROLL + j, j)
            return c
        lax.fori_loop(0, rows // FFN_UNROLL, body, 0)

    @pl.when((f == 0) & (block == 0))
    def _():
        for_rows(lambda s, j: x_gather(base, s).start(priority=1))

    @pl.when(f == 0)
    def _():
        for_rows(lambda s, j: x_gather(base, s).wait())
        for a, col in enumerate(_load_token_major(xrows_ref, rows)):
            xe_ref[:, a * LANES:(a + 1) * LANES] = col.astype(BF16)

    @pl.when((f == 1) & (block + 1 < pl.num_programs(0) * nr))
    def _():
        for_rows(lambda s, j: x_gather(base + rows, s).start(priority=1))

    @pl.when(last_f)
    def _():
        for_rows(lambda s, j: y_gather(s).start(priority=1))

    x = xe_ref[...]
    g = jnp.dot(x, wg_ref[...].astype(BF16), preferred_element_type=F32)
    u = jnp.dot(x, wu_ref[...].astype(BF16), preferred_element_type=F32)
    hid = (jax.nn.silu(g) * u).astype(BF16)
    wd = wd_ref[...].astype(BF16)

    def down(accumulate):
        for c in range(D_MODEL // FFN_DC):
            cs = slice(c * FFN_DC, (c + 1) * FFN_DC)
            part = jnp.dot(hid, wd[:, cs], preferred_element_type=F32)
            acc_ref[:, cs] = acc_ref[:, cs] + part if accumulate else part

    pl.when(f == 0)(lambda: down(False))
    pl.when(f != 0)(lambda: down(True))

    @pl.when(last_f)
    def _():
        for_rows(lambda s, j: y_gather(s).wait())
        gate = gate_ref[...]
        for a in range(TOK_ROWS):
            group = pl.ds(a, rows, stride=TOK_PITCH)
            yrows_ref[group, :] = yrows_ref[group, :] + acc_ref[:, a * LANES:(a + 1) * LANES] * gate
        for_rows(lambda s, j: scatter(s).start(priority=j % 2))
        for_rows(lambda s, j: scatter(s).wait())


def _ffn_call(idx, gates, hn, h, w_gate, w_up, w_down):
    E, cap = idx.shape
    D = D_MODEL
    F = w_gate.shape[2]
    rows = min(FFN_ROWS, cap)
    tokbuf = pltpu.VMEM((rows * TOK_PITCH, LANES), F32)
    grid_spec = pltpu.PrefetchScalarGridSpec(
        num_scalar_prefetch=1,
        grid=(E, cap // rows, F // FFN_FC),
        in_specs=[
            pl.BlockSpec(memory_space=pl.ANY),
            pl.BlockSpec(memory_space=pl.ANY),
            pl.BlockSpec((None, rows, 1), lambda e, r, f, idx: (e, r, 0)),
            pl.BlockSpec((None, D, FFN_FC), lambda e, r, f, idx: (e, 0, f)),
            pl.BlockSpec((None, D, FFN_FC), lambda e, r, f, idx: (e, 0, f)),
            pl.BlockSpec((None, FFN_FC, D), lambda e, r, f, idx: (e, f, 0)),
        ],
        out_specs=pl.BlockSpec(memory_space=pl.ANY),
        scratch_shapes=[tokbuf, tokbuf, pltpu.VMEM((rows, D), BF16), pltpu.VMEM((rows, D), F32),
                        pltpu.SemaphoreType.DMA((3,))],
    )
    assert F // FFN_FC >= 2, "the next block's row gather is issued from hidden chunk 1"
    return pl.pallas_call(
        functools.partial(_ffn_body, cap=cap, rows=rows),
        grid_spec=grid_spec,
        out_shape=jax.ShapeDtypeStruct(h.shape, F32),
        input_output_aliases={2: 0},
        compiler_params=_cparams(("arbitrary", "arbitrary", "arbitrary")),
        name="ec_ffn",
    )(idx.reshape(-1), hn, h, gates, w_gate, w_up, w_down)


def _row_major_body(y_ref, o_ref):
    for a, col in enumerate(_load_token_major(y_ref, o_ref.shape[0])):
        o_ref[:, a * LANES:(a + 1) * LANES] = col


def _row_major_call(y, B, S, tm):
    nblk = S // tm
    return pl.pallas_call(
        _row_major_body,
        grid=(B, nblk),
        in_specs=[pl.BlockSpec((tm * TOK_PITCH, LANES), lambda b, i: (b * nblk + i, 0))],
        out_specs=pl.BlockSpec((None, tm, D_MODEL), lambda b, i: (b, i, 0)),
        out_shape=jax.ShapeDtypeStruct((B, S, D_MODEL), F32),
        compiler_params=_cparams(("parallel", "arbitrary")),
        name="row_major",
    )(y)


def _rope_tables(S):
    t = np.arange(S)
    row = (t // GRID_W).astype(np.float32)
    col = (t % GRID_W).astype(np.float32)
    nf = ROPE_DIM // 4
    inv = (1.0 / (ROPE_BASE ** (jnp.arange(nf, dtype=F32) / nf)))
    ar = jnp.asarray(row)[:, None] * inv
    ac = jnp.asarray(col)[:, None] * inv
    cr, sr, cc, sc = jnp.cos(ar), jnp.sin(ar), jnp.cos(ac), jnp.sin(ac)
    z = jnp.zeros((S, LANES - ROPE_DIM), F32)
    cos = jnp.concatenate([cr, cr, cc, cc, z], axis=1)
    sin = jnp.concatenate([-sr, sr, -sc, sc, z], axis=1)
    return cos, sin


def _pad_lanes(v, fill=0.0):
    return jnp.concatenate([v, jnp.full((LANES - v.shape[0],), fill, v.dtype)])[None, :]


def _prepare(S, norm1_g, w_in, na_q_norm, na_k_norm, na_rpb, q_a_norm, w_q_up, kv_a_norm, w_kv_up,
             mla_qn_norm, mla_qr_norm, mla_kn_norm, mla_kr_norm, na_out_norm, mla_out_norm, w_out,
             norm2_g, w_router):
    p = {}
    p['g1'] = norm1_g[None, :]
    p['win'] = jnp.concatenate(
        [w_in, jnp.zeros((D_MODEL, ROPE_PAD - ROPE_DIM), w_in.dtype)], axis=1).astype(BF16)
    wq = w_q_up.reshape(Q_LORA, MLA_HEADS, QK_DIM)
    wq_nope = wq[:, :, :NOPE_DIM].reshape(Q_LORA, MLA_HEADS * NOPE_DIM)
    wq_pe = jnp.pad(wq[:, :, NOPE_DIM:], ((0, 0), (0, 0), (0, LANES - ROPE_DIM)))
    p['wq'] = jnp.concatenate([wq_nope, wq_pe.reshape(Q_LORA, MLA_HEADS * LANES)], axis=1).astype(BF16)
    wkv = w_kv_up.reshape(KV_LORA, MLA_HEADS, NOPE_DIM + V_DIM)
    p['wkv'] = jnp.concatenate([wkv[:, :, :NOPE_DIM].reshape(KV_LORA, -1),
                                wkv[:, :, NOPE_DIM:].reshape(KV_LORA, -1)], axis=1).astype(BF16)
    p['gq'] = (na_q_norm * (NA_HEAD_DIM ** -0.5 * LOG2E))[None, :]
    p['gk'] = na_k_norm[None, :]
    p['gqa'] = q_a_norm[None, :]
    p['gkva'] = kv_a_norm[None, :]
    p['gkr'] = _pad_lanes(mla_kr_norm)
    p['gqn'] = (mla_qn_norm * (QK_DIM ** -0.5 * LOG2E))[None, :]
    p['gqr'] = _pad_lanes(mla_qr_norm * (QK_DIM ** -0.5 * LOG2E))
    p['gkn'] = mla_kn_norm[None, :]
    p['cos'], p['sin'] = _rope_tables(S)
    p['bias'] = _na_bias_table(na_rpb * LOG2E)
    p['gna'] = na_out_norm[None, :]
    p['gmla'] = mla_out_norm[None, :]
    p['wout'] = w_out.astype(BF16)
    p['g2'] = norm2_g[None, :]
    p['wr'] = jnp.pad(w_router, ((0, 0), (0, LANES - N_EXPERTS))).astype(BF16)
    return p


def _attention_block(x, p):
    qna, kna, vna, qn, qpe, kn, vm, kpe = _proj_call(x, p, tm=512)
    out_na = _na_call(qna, kna, vna, p['bias'])
    out_mla = _mla_call(qn, qpe, kn, kpe, vm, tq=x.shape[1])
    return _out_call(out_na, out_mla, x, p, tm=512)


def _layer(x, p, w_gate, w_up, w_down):
    B, S, D = x.shape
    h, hn, aff_t = _attention_block(x, p)
    idx, gates = _topk_call(aff_t)
    y = _ffn_call(idx, gates, hn, h, w_gate, w_up, w_down)
    return _row_major_call(y, B, S, tm=512)


def kernel(x_prompt, x_sample, norm1_g, w_in, na_q_norm, na_k_norm, na_rpb, q_a_norm, w_q_up, kv_a_norm, w_kv_up, mla_qn_norm, mla_qr_norm, mla_kn_norm, mla_kr_norm, na_out_norm, mla_out_norm, w_out, norm2_g, w_router, w_gate, w_up, w_down):
    hp, hs = x_prompt, x_sample
    for l in range(norm1_g.shape[0]):
        p = _prepare(hp.shape[1], norm1_g[l], w_in[l], na_q_norm[l], na_k_norm[l], na_rpb[l],
                     q_a_norm[l], w_q_up[l], kv_a_norm[l], w_kv_up[l], mla_qn_norm[l],
                     mla_qr_norm[l], mla_kn_norm[l], mla_kr_norm[l], na_out_norm[l],
                     mla_out_norm[l], w_out[l], norm2_g[l], w_router[l])
        hp = _layer(hp, p, w_gate[l], w_up[l], w_down[l])
        hs = _layer(hs, p, w_gate[l], w_up[l], w_down[l])
    return (hp, hs)
```

```python
import functools

import jax
import jax.numpy as jnp
import numpy as np
from jax import lax
from jax.experimental import pallas as pl
from jax.experimental.pallas import tpu as pltpu

D_MODEL = 2048
GRID_W = 64
NA_HEADS = 8
NA_HEAD_DIM = 128
NA_WIN_H = 8
NA_WIN_W = 16
MLA_HEADS = 8
Q_LORA = 512
KV_LORA = 512
NOPE_DIM = 128
ROPE_DIM = 64
V_DIM = 128
ROPE_BASE = 10000.0
N_EXPERTS = 16
EC_CAPACITY = 2
EXPERT_FF = 1024
EPS = 1e-6

NA_WIDTH = NA_HEADS * NA_HEAD_DIM
MLA_WIDTH = MLA_HEADS * V_DIM
QK_DIM = NOPE_DIM + ROPE_DIM
LANES = 128
ROPE_PAD = LANES
P_PAD = 3 * NA_WIDTH + Q_LORA + KV_LORA + ROPE_PAD
NEG = -1e30
LOG2E = 1.4426950408889634

BF16 = jnp.bfloat16
F32 = jnp.float32

VMEM_LIMIT = 56 * 1024 * 1024


def _cparams(sem):
    return pltpu.CompilerParams(dimension_semantics=sem, vmem_limit_bytes=VMEM_LIMIT)


def _resident(shape):
    nd = len(shape)
    return pl.BlockSpec(shape, lambda *_: (0,) * nd, pipeline_mode=pl.Buffered(1))


def _rms(a, width):
    return lax.rsqrt(jnp.sum(a * a, axis=-1, keepdims=True) * (1.0 / width) + EPS)


def _rope(a, cos, sin):
    lane = lax.broadcasted_iota(jnp.int32, a.shape, 1)
    up = pltpu.roll(a, LANES - 16, axis=1)
    dn = pltpu.roll(a, 16, axis=1)
    partner = jnp.where((lane % 32) < 16, up, dn)
    return a * cos + partner * sin


def _proj_body(x_ref, g1_ref, win_ref, wq_ref, wkv_ref, gq_ref, gk_ref, gqa_ref, gkva_ref,
               gkr_ref, gqn_ref, gqr_ref, gkn_ref, cos_ref, sin_ref,
               qna_ref, kna_ref, vna_ref, qn_ref, qpe_ref, kn_ref, vm_ref, kpe_ref):
    half = NA_WIDTH // 2
    nh = MLA_HEADS * LANES

    def chain(rows):
        x = x_ref[rows, :]
        xn = (x * _rms(x, D_MODEL) * g1_ref[...]).astype(BF16)
        cos = cos_ref[rows, :]
        sin = sin_ref[rows, :]

        def proj(lo, hi):
            return jnp.dot(xn, win_ref[:, lo:hi], preferred_element_type=F32)

        def heads_to(ref, a, g, width, base, rope=False):
            for h in range(a.shape[1] // LANES):
                ah = a[:, h * LANES:(h + 1) * LANES]
                y = ah * _rms(ah, width) * g
                if rope:
                    y = _rope(y, cos, sin)
                ref[rows, base + h * LANES: base + (h + 1) * LANES] = y.astype(BF16)

        for c in range(2):
            heads_to(qna_ref, proj(c * half, (c + 1) * half), gq_ref[...], NA_HEAD_DIM, c * half)
        for c in range(2):
            heads_to(kna_ref, proj(NA_WIDTH + c * half, NA_WIDTH + (c + 1) * half), gk_ref[...],
                     NA_HEAD_DIM, c * half)
        for c in range(2):
            vna_ref[rows, c * half:(c + 1) * half] = proj(2 * NA_WIDTH + c * half,
                                                          2 * NA_WIDTH + (c + 1) * half).astype(BF16)

        o = 3 * NA_WIDTH
        ql = proj(o, o + Q_LORA)
        qln = (ql * _rms(ql, Q_LORA) * gqa_ref[...]).astype(BF16)
        kvl = proj(o + Q_LORA, o + Q_LORA + KV_LORA)
        kvln = (kvl * _rms(kvl, KV_LORA) * gkva_ref[...]).astype(BF16)
        kp = proj(o + Q_LORA + KV_LORA, P_PAD)
        heads_to(kpe_ref, kp, gkr_ref[...], ROPE_DIM, 0, rope=True)

        for c in range(2):
            a = jnp.dot(qln, wq_ref[:, c * half:(c + 1) * half], preferred_element_type=F32)
            heads_to(qn_ref, a, gqn_ref[...], NOPE_DIM, c * half)
        for c in range(2):
            a = jnp.dot(qln, wq_ref[:, nh + c * half: nh + (c + 1) * half], preferred_element_type=F32)
            heads_to(qpe_ref, a, gqr_ref[...], ROPE_DIM, c * half, rope=True)
        for c in range(2):
            a = jnp.dot(kvln, wkv_ref[:, c * half:(c + 1) * half], preferred_element_type=F32)
            heads_to(kn_ref, a, gkn_ref[...], NOPE_DIM, c * half)
        for c in range(2):
            vm_ref[rows, c * half:(c + 1) * half] = jnp.dot(
                kvln, wkv_ref[:, nh + c * half: nh + (c + 1) * half],
                preferred_element_type=F32).astype(BF16)

    for c in range(x_ref.shape[0] // PROJ_SUB):
        chain(slice(c * PROJ_SUB, (c + 1) * PROJ_SUB))


def _proj_call(x, p, tm):
    B, S, D = x.shape
    grid = (B, S // tm)
    row = lambda w: pl.BlockSpec((None, tm, w), lambda b, i: (b, i, 0))
    tab = pl.BlockSpec((tm, LANES), lambda b, i: (i, 0))
    vec = lambda w: _resident((1, w))
    out_w = (NA_WIDTH, NA_WIDTH, NA_WIDTH, MLA_HEADS * LANES, MLA_HEADS * LANES,
             MLA_HEADS * LANES, MLA_WIDTH, LANES)
    return pl.pallas_call(
        _proj_body,
        grid=grid,
        in_specs=[row(D), vec(D), _resident((D, P_PAD)), _resident(p['wq'].shape),
                  _resident(p['wkv'].shape), vec(LANES), vec(LANES), vec(Q_LORA), vec(KV_LORA),
                  vec(LANES), vec(LANES), vec(LANES), vec(LANES), tab, tab],
        out_specs=[row(w) for w in out_w],
        out_shape=[jax.ShapeDtypeStruct((B, S, w), BF16) for w in out_w],
        compiler_params=_cparams(("parallel", "arbitrary")),
        name="in_proj",
    )(x, p['g1'], p['win'], p['wq'], p['wkv'], p['gq'], p['gk'], p['gqa'], p['gkva'],
      p['gkr'], p['gqn'], p['gqr'], p['gkn'], p['cos'], p['sin'])


NA_QROWS = 4
NA_KROWS = NA_QROWS + NA_WIN_H


def _na_window_start(i, rows):
    return int(np.clip(NA_QROWS * i - NA_WIN_H // 2, 0, rows - NA_KROWS))


def _na_build_bias(tab_ref, bias_ref, rows):
    kh = min(NA_WIN_H, rows)
    shape = (GRID_W, LANES)
    c = lax.broadcasted_iota(jnp.int32, shape, 0)
    l = lax.broadcasted_iota(jnp.int32, shape, 1)
    cs = jnp.clip(c - NA_WIN_W // 2, 0, GRID_W - NA_WIN_W)
    in_lo = (l >= cs) & (l < cs + NA_WIN_W)
    in_hi = (l - GRID_W >= cs) & (l - GRID_W < cs + NA_WIN_W)
    neg = jnp.full(shape, NEG, F32)
    lo, hi = [], []
    for d in range(2 * NA_WIN_H - 1):
        base = jnp.broadcast_to(tab_ref[d:d + 1, :], shape)
        lo.append(jnp.where(in_lo, pltpu.roll(base, 0, 1, stride=1, stride_axis=0), NEG))
        hi.append(jnp.where(in_hi, pltpu.roll(base, GRID_W, 1, stride=1, stride_axis=0), NEG))
    for i in range(rows // NA_QROWS):
        ws = _na_window_start(i, rows)
        for qr in range(NA_QROWS):
            r = NA_QROWS * i + qr
            rs = int(np.clip(r - kh // 2, 0, rows - kh))
            for m in range(NA_KROWS // 2):
                k0 = ws + 2 * m
                a = lo[k0 - r + NA_WIN_H - 1] if rs <= k0 < rs + kh else neg
                b = hi[k0 + 1 - r + NA_WIN_H - 1] if rs <= k0 + 1 < rs + kh else neg
                bias_ref[i, qr * GRID_W:(qr + 1) * GRID_W, m * LANES:(m + 1) * LANES] = (
                    jnp.where(l < GRID_W, a, b))


def _na_body(q_ref, k_ref, v_ref, tab_ref, o_ref, bias_ref, vone_ref, *, rows):
    tq = NA_QROWS * GRID_W
    tk = NA_KROWS * GRID_W

    @pl.when(pl.program_id(1) == 0)
    def _():
        _na_build_bias(tab_ref, bias_ref, rows)

    vone_ref[:, :LANES] = v_ref[...]
    vone_ref[:, LANES:] = jnp.ones(v_ref.shape, BF16)

    for i in range(rows // NA_QROWS):
        ws = _na_window_start(i, rows) * GRID_W
        k = k_ref[ws:ws + tk, :]
        v_ones = vone_ref[ws:ws + tk, :]
        for c in range(tq // ATTN_SUB):
            lo = i * tq + c * ATTN_SUB
            s = _dot_t(q_ref[lo:lo + ATTN_SUB, :], k) + bias_ref[i, c * ATTN_SUB:(c + 1) * ATTN_SUB, :]
            o_ref[lo:lo + ATTN_SUB, :] = _softmax_pv(s, v_ones).astype(BF16)


def _na_bias_table(rpb):
    H, nd, nc = rpb.shape
    half = nc // 2
    fill = jnp.full((H, nd, LANES - nc), NEG, rpb.dtype)
    tab = jnp.concatenate([rpb[:, :, half:], fill, rpb[:, :, :half]], axis=2)
    return jnp.concatenate([tab, jnp.full((H, 16 - nd, LANES), NEG, rpb.dtype)], axis=1)


def _na_call(q, k, v, tab):
    B, S, _ = q.shape
    rows = S // GRID_W
    hb = lambda: pl.BlockSpec((None, S, LANES), lambda h, b: (b, 0, h))
    return pl.pallas_call(
        functools.partial(_na_body, rows=rows),
        grid=(NA_HEADS, B),
        in_specs=[hb(), hb(), hb(),
                  pl.BlockSpec((None,) + tab.shape[1:], lambda h, b: (h, 0, 0))],
        out_specs=hb(),
        out_shape=jax.ShapeDtypeStruct((B, S, NA_WIDTH), BF16),
        scratch_shapes=[pltpu.VMEM((rows // NA_QROWS, NA_QROWS * GRID_W, NA_KROWS * GRID_W), F32),
                        pltpu.VMEM((S, 2 * LANES), BF16)],
        compiler_params=_cparams(("arbitrary", "arbitrary")),
        name="na_attn",
    )(q, k, v, tab)


ATTN_SUB = 256


def _softmax_pv(s, v_ones):
    m = jnp.max(s, axis=-1, keepdims=True)
    e = jnp.exp2(s - m).astype(BF16)
    ol = jnp.dot(e, v_ones, preferred_element_type=F32)
    return ol[:, :LANES] / ol[:, LANES:]


def _mla_body(qn_ref, qpe_ref, kn_ref, kpe_ref, v_ref, o_ref, kcat_ref, vone_ref):
    @pl.when(pl.program_id(2) == 0)
    def _():
        kcat_ref[:, :LANES] = kn_ref[...]
        kcat_ref[:, LANES:] = kpe_ref[...]
        vone_ref[:, :LANES] = v_ref[...]
        vone_ref[:, LANES:] = jnp.ones(v_ref.shape, BF16)

    for c in range(qn_ref.shape[0] // ATTN_SUB):
        rows = slice(c * ATTN_SUB, (c + 1) * ATTN_SUB)
        q = jnp.concatenate([qn_ref[rows, :], qpe_ref[rows, :]], axis=1)
        s = _dot_t(q, kcat_ref[...])
        o_ref[rows, :] = _softmax_pv(s, vone_ref[...]).astype(BF16)


def _mla_call(qn, qpe, kn, kpe, v, tq):
    B, S, _ = qn.shape
    qb = lambda: pl.BlockSpec((None, tq, LANES), lambda b, h, i: (b, i, h))
    kb = lambda: pl.BlockSpec((None, S, LANES), lambda b, h, i: (b, 0, h))
    return pl.pallas_call(
        _mla_body,
        grid=(B, MLA_HEADS, S // tq),
        in_specs=[qb(), qb(), kb(), pl.BlockSpec((None, S, LANES), lambda b, h, i: (b, 0, 0)), kb()],
        out_specs=qb(),
        out_shape=jax.ShapeDtypeStruct((B, S, MLA_WIDTH), BF16),
        scratch_shapes=[pltpu.VMEM((S, 2 * LANES), BF16), pltpu.VMEM((S, 2 * LANES), BF16)],
        compiler_params=_cparams(("parallel", "arbitrary", "arbitrary")),
        name="mla_attn",
    )(qn, qpe, kn, kpe, v)


TOK_ROWS = D_MODEL // LANES
TOK_PITCH = TOK_ROWS + 1


PROJ_SUB = 256


def _store_token_major(ref, val, first_token=0):
    t = val.shape[0]
    base = first_token * TOK_PITCH
    for a in range(TOK_ROWS):
        ref[pl.ds(base + a, t, stride=TOK_PITCH), :] = val[:, a * LANES:(a + 1) * LANES]
    ref[pl.ds(base + TOK_ROWS, t, stride=TOK_PITCH), :] = jnp.zeros((t, LANES), val.dtype)


def _load_token_major(ref, t):
    return [ref[pl.ds(a, t, stride=TOK_PITCH), :] for a in range(TOK_ROWS)]


def _out_body(na_ref, mla_ref, x_ref, gna_ref, gmla_ref, wout_ref, g2_ref, wr_ref,
              h_ref, hn_ref, aff_ref):
    for c in range(x_ref.shape[0] // PROJ_SUB):
        rows = slice(c * PROJ_SUB, (c + 1) * PROJ_SUB)
        a = na_ref[rows, :].astype(F32)
        an = (a * _rms(a, NA_WIDTH) * gna_ref[...]).astype(BF16)
        b = mla_ref[rows, :].astype(F32)
        bn = (b * _rms(b, MLA_WIDTH) * gmla_ref[...]).astype(BF16)
        h = x_ref[rows, :]
        h = h + jnp.dot(an, wout_ref[:NA_WIDTH, :], preferred_element_type=F32)
        h = h + jnp.dot(bn, wout_ref[NA_WIDTH:, :], preferred_element_type=F32)
        hn = h * _rms(h, D_MODEL) * g2_ref[...]
        _store_token_major(h_ref, h, c * PROJ_SUB)
        _store_token_major(hn_ref, hn, c * PROJ_SUB)
        logits = jnp.dot(hn.astype(BF16), wr_ref[...], preferred_element_type=F32)
        lane = lax.broadcasted_iota(jnp.int32, logits.shape, 1)
        logits = jnp.where(lane < N_EXPERTS, logits, NEG)
        m = jnp.max(logits, axis=-1, keepdims=True)
        e = jnp.exp(logits - m)
        aff = e / jnp.sum(e, axis=-1, keepdims=True)
        aff_ref[:, rows] = aff.T[:N_EXPERTS, :]


def _out_call(na, mla, x, p, tm):
    B, S, D = x.shape
    row = lambda w: pl.BlockSpec((None, tm, w), lambda b, i: (b, i, 0))
    nblk = S // tm
    tokmaj = lambda: pl.BlockSpec((tm * TOK_PITCH, LANES), lambda b, i: (b * nblk + i, 0))
    return pl.pallas_call(
        _out_body,
        grid=(B, nblk),
        in_specs=[row(NA_WIDTH), row(MLA_WIDTH), row(D), _resident((1, NA_WIDTH)),
                  _resident((1, MLA_WIDTH)), _resident((NA_WIDTH + MLA_WIDTH, D)),
                  _resident((1, D)), _resident((D, LANES))],
        out_specs=[tokmaj(), tokmaj(), pl.BlockSpec((N_EXPERTS, tm), lambda b, i: (0, b * nblk + i))],
        out_shape=[jax.ShapeDtypeStruct((B * S * TOK_PITCH, LANES), F32),
                   jax.ShapeDtypeStruct((B * S * TOK_PITCH, LANES), F32),
                   jax.ShapeDtypeStruct((N_EXPERTS, B * S), F32)],
        compiler_params=_cparams(("parallel", "arbitrary")),
        name="out_proj",
    )(na, mla, x, p['gna'], p['gmla'], p['wout'], p['g2'], p['wr'])


def _tri(n, strict, upper):
    r = lax.broadcasted_iota(jnp.int32, (n, n), 0)
    c = lax.broadcasted_iota(jnp.int32, (n, n), 1)
    if upper:
        m = (r < c) if strict else (r <= c)
    else:
        m = (c < r) if strict else (c <= r)
    return jnp.where(m, 1.0, 0.0).astype(BF16)


def _dot_t(a, b):
    return lax.dot_general(a, b, (((1,), (1,)), ((), ())), preferred_element_type=F32)


def _topk_body(aff_ref, idx_ref, gate_ref, sel_ref, cum_ref, *, nb, cap):
    E = N_EXPERTS
    aff = aff_ref[...]
    bits = pltpu.bitcast(aff, jnp.int32)

    def count(mask):
        c = jnp.sum(jnp.where(mask, 1.0, 0.0), axis=1, keepdims=True)
        return jnp.sum(c, axis=2, keepdims=True)

    def search(b, t):
        cand = t | jnp.left_shift(jnp.int32(1), 30 - b)
        return jnp.where(count(bits >= cand) >= cap, cand, t)

    t = lax.fori_loop(0, 31, search, jnp.zeros((E, 1, 1), jnp.int32))

    upper_incl = _tri(LANES, strict=False, upper=True)
    lower_strict = jnp.broadcast_to(_tri(nb, strict=True, upper=False)[None], (E, nb, nb))

    def cumsum_incl(m):
        mb = m.astype(BF16)
        within = jnp.dot(mb.reshape(E * nb, LANES), upper_incl,
                         preferred_element_type=F32).reshape(E, nb, LANES)
        before = lax.dot_general(lower_strict, mb, (((2,), (1,)), ((0,), (0,))),
                                 preferred_element_type=F32)
        return within + jnp.sum(before, axis=2, keepdims=True)

    gt = bits > t
    eq = jnp.where(bits == t, 1.0, 0.0)
    need = cap - count(gt)
    eq_before = cumsum_incl(eq) - eq
    sel = jnp.where(gt | ((eq > 0.5) & (eq_before < need)), 1.0, 0.0)
    sel_ref[...] = sel
    cum_ref[...] = cumsum_incl(sel)

    ones_rows = jnp.ones((8, LANES), BF16)
    lane0 = jnp.where(lax.broadcasted_iota(jnp.int32, (8, LANES), 1) == 0, 1.0, 0.0).astype(BF16)
    ones_nb = jnp.ones((nb, LANES), BF16)
    ones_sq = jnp.ones((LANES, LANES), BF16)
    upper_strict_nb = _tri(nb, strict=True, upper=True)
    slot_nb = lax.broadcasted_iota(jnp.int32, (cap, nb), 0).astype(F32)
    slot = lax.broadcasted_iota(jnp.int32, (cap, LANES), 0).astype(F32)
    lane = lax.broadcasted_iota(jnp.int32, (cap, LANES), 1).astype(F32)

    def compact(e, carry):
        m = sel_ref[e].astype(BF16)
        c = cum_ref[e]
        row_tot = _dot_t(ones_rows, m)
        row_off = jnp.dot(row_tot.astype(BF16), upper_strict_nb, preferred_element_type=F32)
        row_end = row_off + row_tot
        off_b = row_off[0:1, :]
        end_b = row_end[0:1, :]
        onehot = jnp.where((off_b <= slot_nb) & (slot_nb < end_b), 1.0, 0.0).astype(BF16)
        rows_before = jnp.where(end_b <= slot_nb, 1.0, 0.0).astype(BF16)
        row_idx = jnp.dot(rows_before, ones_nb, preferred_element_type=F32)
        c_hi = jnp.floor(c * (1.0 / LANES))
        c_lo = c - c_hi * LANES
        c_row = (jnp.dot(onehot, c_hi.astype(BF16), preferred_element_type=F32) * LANES
                 + jnp.dot(onehot, c_lo.astype(BF16), preferred_element_type=F32))
        before = jnp.where(c_row <= slot, 1.0, 0.0).astype(BF16)
        lane_idx = jnp.dot(before, ones_sq, preferred_element_type=F32)
        tok = (_dot_t(lane0, row_idx.astype(BF16)) * LANES
               + _dot_t(lane0, lane_idx.astype(BF16)))
        idx_ref[e] = tok.astype(jnp.int32)
        a = aff_ref[e]
        a1 = a.astype(BF16)
        r1 = a - a1.astype(F32)
        a2 = r1.astype(BF16)
        a3 = (r1 - a2.astype(F32)).astype(BF16)
        a_row = (jnp.dot(onehot, a1, preferred_element_type=F32)
                 + jnp.dot(onehot, a2, preferred_element_type=F32)
                 + jnp.dot(onehot, a3, preferred_element_type=F32))
        gate_ref[e] = jnp.sum(jnp.where(lane == lane_idx, a_row, 0.0), axis=1, keepdims=True)
        return carry

    lax.fori_loop(0, E, compact, 0)


def _topk_call(aff_t):
    E, N = aff_t.shape
    nb = N // LANES
    cap = EC_CAPACITY * N // N_EXPERTS
    idx, gates = pl.pallas_call(
        functools.partial(_topk_body, nb=nb, cap=cap),
        out_shape=[jax.ShapeDtypeStruct((E, 8, cap), jnp.int32),
                   jax.ShapeDtypeStruct((E, cap, 1), F32)],
        scratch_shapes=[pltpu.VMEM((E, nb, LANES), F32), pltpu.VMEM((E, nb, LANES), F32)],
        compiler_params=pltpu.CompilerParams(vmem_limit_bytes=VMEM_LIMIT),
        name="ec_topk",
    )(aff_t.reshape(E, nb, LANES))
    return idx[:, 0, :], gates


FFN_ROWS = 1024
FFN_FC = 256
FFN_DC = 512
FFN_UNROLL = 8


def _ffn_body(idx_ref, hn_hbm, yin_hbm, gate_ref, wg_ref, wu_ref, wd_ref, y_hbm,
              xrows_ref, yrows_ref, xe_ref, acc_ref, sem, *, cap, rows):
    del yin_hbm
    e, r, f = pl.program_id(0), pl.program_id(1), pl.program_id(2)
    nr, nf = pl.num_programs(1), pl.num_programs(2)
    nblocks = pl.num_programs(0) * nr
    block = e * nr + r
    base = block * rows
    last_f = f == nf - 1

    def hbm_token(ref, slot):
        return ref.at[pl.ds(idx_ref[slot] * TOK_PITCH, TOK_ROWS)]

    def vmem_token(ref, s):
        return ref.at[pl.ds(s * TOK_PITCH, TOK_ROWS)]

    def x_gather(first_slot, s):
        return pltpu.make_async_copy(hbm_token(hn_hbm, first_slot + s), vmem_token(xrows_ref, s), sem.at[0])

    def y_gather(s):
        return pltpu.make_async_copy(hbm_token(y_hbm, base + s), vmem_token(yrows_ref, s), sem.at[1])

    def scatter(first_slot, s):
        return pltpu.make_async_copy(vmem_token(yrows_ref, s), hbm_token(y_hbm, first_slot + s), sem.at[2])

    def for_rows(fn):
        def body(k, c):
            for j in range(FFN_UNROLL):
                fn(k * FFN_UNROLL + j, j)
            return c
        lax.fori_loop(0, rows // FFN_UNROLL, body, 0)

    @pl.when((f == 0) & (block == 0))
    def _():
        for_rows(lambda s, j: x_gather(base, s).start(priority=1))

    @pl.when(f == 0)
    def _():
        for_rows(lambda s, j: x_gather(base, s).wait())
        for a, col in enumerate(_load_token_major(xrows_ref, rows)):
            xe_ref[:, a * LANES:(a + 1) * LANES] = col.astype(BF16)

    @pl.when((f == 1) & (block > 0))
    def _():
        for_rows(lambda s, j: scatter(base - rows, s).wait())

    @pl.when(f == 1)
    def _():
        for_rows(lambda s, j: y_gather(s).start(priority=1))

    @pl.when((f == 1) & (block + 1 < nblocks))
    def _():
        for_rows(lambda s, j: x_gather(base + rows, s).start(priority=1))

    x = xe_ref[...]
    g = jnp.dot(x, wg_ref[...].astype(BF16), preferred_element_type=F32)
    u = jnp.dot(x, wu_ref[...].astype(BF16), preferred_element_type=F32)
    hid = (jax.nn.silu(g) * u).astype(BF16)
    wd = wd_ref[...].astype(BF16)

    def down(accumulate):
        for c in range(D_MODEL // FFN_DC):
            cs = slice(c * FFN_DC, (c + 1) * FFN_DC)
            part = jnp.dot(hid, wd[:, cs], preferred_element_type=F32)
            acc_ref[:, cs] = acc_ref[:, cs] + part if accumulate else part

    pl.when(f == 0)(lambda: down(False))
    pl.when(f != 0)(lambda: down(True))

    @pl.when(last_f)
    def _():
        for_rows(lambda s, j: y_gather(s).wait())
        gate = gate_ref[...]
        for a in range(TOK_ROWS):
            group = pl.ds(a, rows, stride=TOK_PITCH)
            yrows_ref[group, :] = yrows_ref[group, :] + acc_ref[:, a * LANES:(a + 1) * LANES] * gate
        for_rows(lambda s, j: scatter(base, s).start(priority=j % 2))

    @pl.when(last_f & (block == nblocks - 1))
    def _():
        for_rows(lambda s, j: scatter(base, s).wait())


def _ffn_call(idx, gates, hn, h, w_gate, w_up, w_down):
    E, cap = idx.shape
    D = D_MODEL
    F = w_gate.shape[2]
    rows = min(FFN_ROWS, cap)
    tokbuf = pltpu.VMEM((rows * TOK_PITCH, LANES), F32)
    grid_spec = pltpu.PrefetchScalarGridSpec(
        num_scalar_prefetch=1,
        grid=(E, cap // rows, F // FFN_FC),
        in_specs=[
            pl.BlockSpec(memory_space=pl.ANY),
            pl.BlockSpec(memory_space=pl.ANY),
            pl.BlockSpec((None, rows, 1), lambda e, r, f, idx: (e, r, 0)),
            pl.BlockSpec((None, D, FFN_FC), lambda e, r, f, idx: (e, 0, f)),
            pl.BlockSpec((None, D, FFN_FC), lambda e, r, f, idx: (e, 0, f)),
            pl.BlockSpec((None, FFN_FC, D), lambda e, r, f, idx: (e, f, 0)),
        ],
        out_specs=pl.BlockSpec(memory_space=pl.ANY),
        scratch_shapes=[tokbuf, tokbuf, pltpu.VMEM((rows, D), BF16), pltpu.VMEM((rows, D), F32),
                        pltpu.SemaphoreType.DMA((3,))],
    )
    assert F // FFN_FC >= 2, "the next block's row gather is issued from hidden chunk 1"
    return pl.pallas_call(
        functools.partial(_ffn_body, cap=cap, rows=rows),
        grid_spec=grid_spec,
        out_shape=jax.ShapeDtypeStruct(h.shape, F32),
        input_output_aliases={2: 0},
        compiler_params=_cparams(("arbitrary", "arbitrary", "arbitrary")),
        name="ec_ffn",
    )(idx.reshape(-1), hn, h, gates, w_gate, w_up, w_down)


def _row_major_body(y_ref, o_ref):
    for a, col in enumerate(_load_token_major(y_ref, o_ref.shape[0])):
        o_ref[:, a * LANES:(a + 1) * LANES] = col


def _row_major_call(y, B, S, tm):
    nblk = S // tm
    return pl.pallas_call(
        _row_major_body,
        grid=(B, nblk),
        in_specs=[pl.BlockSpec((tm * TOK_PITCH, LANES), lambda b, i: (b * nblk + i, 0))],
        out_specs=pl.BlockSpec((None, tm, D_MODEL), lambda b, i: (b, i, 0)),
        out_shape=jax.ShapeDtypeStruct((B, S, D_MODEL), F32),
        compiler_params=_cparams(("parallel", "arbitrary")),
        name="row_major",
    )(y)


def _rope_tables(S):
    t = np.arange(S)
    row = (t // GRID_W).astype(np.float32)
    col = (t % GRID_W).astype(np.float32)
    nf = ROPE_DIM // 4
    inv = (1.0 / (np.float32(ROPE_BASE) ** (np.arange(nf, dtype=np.float32) / nf))).astype(np.float32)
    ar = (row[:, None] * inv).astype(np.float64)
    ac = (col[:, None] * inv).astype(np.float64)
    cr, sr, cc, sc = np.cos(ar), np.sin(ar), np.cos(ac), np.sin(ac)
    z = np.zeros((S, LANES - ROPE_DIM))
    cos = np.concatenate([cr, cr, cc, cc, z], axis=1).astype(np.float32)
    sin = np.concatenate([-sr, sr, -sc, sc, z], axis=1).astype(np.float32)
    return jnp.asarray(cos), jnp.asarray(sin)


def _pad_lanes(v, fill=0.0):
    return jnp.concatenate([v, jnp.full((LANES - v.shape[0],), fill, v.dtype)])[None, :]


def _prepare(S, norm1_g, w_in, na_q_norm, na_k_norm, na_rpb, q_a_norm, w_q_up, kv_a_norm, w_kv_up,
             mla_qn_norm, mla_qr_norm, mla_kn_norm, mla_kr_norm, na_out_norm, mla_out_norm, w_out,
             norm2_g, w_router):
    p = {}
    p['g1'] = norm1_g[None, :]
    p['win'] = jnp.concatenate(
        [w_in, jnp.zeros((D_MODEL, ROPE_PAD - ROPE_DIM), w_in.dtype)], axis=1).astype(BF16)
    wq = w_q_up.reshape(Q_LORA, MLA_HEADS, QK_DIM)
    wq_nope = wq[:, :, :NOPE_DIM].reshape(Q_LORA, MLA_HEADS * NOPE_DIM)
    wq_pe = jnp.pad(wq[:, :, NOPE_DIM:], ((0, 0), (0, 0), (0, LANES - ROPE_DIM)))
    p['wq'] = jnp.concatenate([wq_nope, wq_pe.reshape(Q_LORA, MLA_HEADS * LANES)], axis=1).astype(BF16)
    wkv = w_kv_up.reshape(KV_LORA, MLA_HEADS, NOPE_DIM + V_DIM)
    p['wkv'] = jnp.concatenate([wkv[:, :, :NOPE_DIM].reshape(KV_LORA, -1),
                                wkv[:, :, NOPE_DIM:].reshape(KV_LORA, -1)], axis=1).astype(BF16)
    p['gq'] = (na_q_norm * (NA_HEAD_DIM ** -0.5 * LOG2E))[None, :]
    p['gk'] = na_k_norm[None, :]
    p['gqa'] = q_a_norm[None, :]
    p['gkva'] = kv_a_norm[None, :]
    p['gkr'] = _pad_lanes(mla_kr_norm)
    p['gqn'] = (mla_qn_norm * (QK_DIM ** -0.5 * LOG2E))[None, :]
    p['gqr'] = _pad_lanes(mla_qr_norm * (QK_DIM ** -0.5 * LOG2E))
    p['gkn'] = mla_kn_norm[None, :]
    p['cos'], p['sin'] = _rope_tables(S)
    p['bias'] = _na_bias_table(na_rpb * LOG2E)
    p['gna'] = na_out_norm[None, :]
    p['gmla'] = mla_out_norm[None, :]
    p['wout'] = w_out.astype(BF16)
    p['g2'] = norm2_g[None, :]
    p['wr'] = jnp.pad(w_router, ((0, 0), (0, LANES - N_EXPERTS))).astype(BF16)
    return p


def _attention_block(x, p):
    qna, kna, vna, qn, qpe, kn, vm, kpe = _proj_call(x, p, tm=512)
    out_na = _na_call(qna, kna, vna, p['bias'])
    out_mla = _mla_call(qn, qpe, kn, kpe, vm, tq=x.shape[1])
    return _out_call(out_na, out_mla, x, p, tm=512)


def _layer(x, p, w_gate, w_up, w_down):
    B, S, D = x.shape
    h, hn, aff_t = _attention_block(x, p)
    idx, gates = _topk_call(aff_t)
    y = _ffn_call(idx, gates, hn, h, w_gate, w_up, w_down)
    return _row_major_call(y, B, S, tm=512)


def kernel(x_prompt, x_sample, norm1_g, w_in, na_q_norm, na_k_norm, na_rpb, q_a_norm, w_q_up, kv_a_norm, w_kv_up, mla_qn_norm, mla_qr_norm, mla_kn_norm, mla_kr_norm, na_out_norm, mla_out_norm, w_out, norm2_g, w_router, w_gate, w_up, w_down):
    hp, hs = x_prompt, x_sample
    for l in range(norm1_g.shape[0]):
        p = _prepare(hp.shape[1], norm1_g[l], w_in[l], na_q_norm[l], na_k_norm[l], na_rpb[l],
                     q_a_norm[l], w_q_up[l], kv_a_norm[l], w_kv_up[l], mla_qn_norm[l],
                     mla_qr_norm[l], mla_kn_norm[l], mla_kr_norm[l], na_out_norm[l],
                     mla_out_norm[l], w_out[l], norm2_g[l], w_router[l])
        hp = _layer(hp, p, w_gate[l], w_up[l], w_down[l])
        hs = _layer(hs, p, w_gate[l], w_up[l], w_down[l])
    return (hp, hs)
```

```python
import functools

import jax
import jax.numpy as jnp
import numpy as np
from jax import lax
from jax.experimental import pallas as pl
from jax.experimental.pallas import tpu as pltpu

D_MODEL = 2048
GRID_W = 64
NA_HEADS = 8
NA_HEAD_DIM = 128
NA_WIN_H = 8
NA_WIN_W = 16
MLA_HEADS = 8
Q_LORA = 512
KV_LORA = 512
NOPE_DIM = 128
ROPE_DIM = 64
V_DIM = 128
ROPE_BASE = 10000.0
N_EXPERTS = 16
EC_CAPACITY = 2
EXPERT_FF = 1024
EPS = 1e-6

NA_WIDTH = NA_HEADS * NA_HEAD_DIM
MLA_WIDTH = MLA_HEADS * V_DIM
QK_DIM = NOPE_DIM + ROPE_DIM
LANES = 128
ROPE_PAD = LANES
P_PAD = 3 * NA_WIDTH + Q_LORA + KV_LORA + ROPE_PAD
NEG = -1e30
LOG2E = 1.4426950408889634

BF16 = jnp.bfloat16
F32 = jnp.float32

VMEM_LIMIT = 56 * 1024 * 1024


def _cparams(sem):
    return pltpu.CompilerParams(dimension_semantics=sem, vmem_limit_bytes=VMEM_LIMIT)


def _resident(shape):
    nd = len(shape)
    return pl.BlockSpec(shape, lambda *_: (0,) * nd, pipeline_mode=pl.Buffered(1))


def _rms(a, width):
    return lax.rsqrt(jnp.sum(a * a, axis=-1, keepdims=True) * (1.0 / width) + EPS)


def _rope(a, cos, sin):
    lane = lax.broadcasted_iota(jnp.int32, a.shape, 1)
    up = pltpu.roll(a, LANES - 16, axis=1)
    dn = pltpu.roll(a, 16, axis=1)
    partner = jnp.where((lane % 32) < 16, up, dn)
    return a * cos + partner * sin


def _proj_body(x_ref, g1_ref, win_ref, wq_ref, wkv_ref, gq_ref, gk_ref, gqa_ref, gkva_ref,
               gkr_ref, gqn_ref, gqr_ref, gkn_ref, cos_ref, sin_ref,
               qna_ref, kna_ref, vna_ref, qn_ref, qpe_ref, kn_ref, vm_ref, kpe_ref):
    half = NA_WIDTH // 2
    nh = MLA_HEADS * LANES

    def chain(rows):
        x = x_ref[rows, :]
        xn = (x * _rms(x, D_MODEL) * g1_ref[...]).astype(BF16)
        cos = cos_ref[rows, :]
        sin = sin_ref[rows, :]

        def proj(lo, hi):
            return jnp.dot(xn, win_ref[:, lo:hi], preferred_element_type=F32)

        def heads_to(ref, a, g, width, base, rope=False):
            for h in range(a.shape[1] // LANES):
                ah = a[:, h * LANES:(h + 1) * LANES]
                y = ah * _rms(ah, width) * g
                if rope:
                    y = _rope(y, cos, sin)
                ref[rows, base + h * LANES: base + (h + 1) * LANES] = y.astype(BF16)

        for c in range(2):
            heads_to(qna_ref, proj(c * half, (c + 1) * half), gq_ref[...], NA_HEAD_DIM, c * half)
        for c in range(2):
            heads_to(kna_ref, proj(NA_WIDTH + c * half, NA_WIDTH + (c + 1) * half), gk_ref[...],
                     NA_HEAD_DIM, c * half)
        for c in range(2):
            vna_ref[rows, c * half:(c + 1) * half] = proj(2 * NA_WIDTH + c * half,
                                                          2 * NA_WIDTH + (c + 1) * half).astype(BF16)

        o = 3 * NA_WIDTH
        ql = proj(o, o + Q_LORA)
        qln = (ql * _rms(ql, Q_LORA) * gqa_ref[...]).astype(BF16)
        kvl = proj(o + Q_LORA, o + Q_LORA + KV_LORA)
        kvln = (kvl * _rms(kvl, KV_LORA) * gkva_ref[...]).astype(BF16)
        kp = proj(o + Q_LORA + KV_LORA, P_PAD)
        heads_to(kpe_ref, kp, gkr_ref[...], ROPE_DIM, 0, rope=True)

        for c in range(2):
            a = jnp.dot(qln, wq_ref[:, c * half:(c + 1) * half], preferred_element_type=F32)
            heads_to(qn_ref, a, gqn_ref[...], NOPE_DIM, c * half)
        for c in range(2):
            a = jnp.dot(qln, wq_ref[:, nh + c * half: nh + (c + 1) * half], preferred_element_type=F32)
            heads_to(qpe_ref, a, gqr_ref[...], ROPE_DIM, c * half, rope=True)
        for c in range(2):
            a = jnp.dot(kvln, wkv_ref[:, c * half:(c + 1) * half], preferred_element_type=F32)
            heads_to(kn_ref, a, gkn_ref[...], NOPE_DIM, c * half)
        for c in range(2):
            vm_ref[rows, c * half:(c + 1) * half] = jnp.dot(
                kvln, wkv_ref[:, nh + c * half: nh + (c + 1) * half],
                preferred_element_type=F32).astype(BF16)

    for c in range(x_ref.shape[0] // PROJ_SUB):
        chain(slice(c * PROJ_SUB, (c + 1) * PROJ_SUB))


def _proj_call(x, p, tm):
    B, S, D = x.shape
    grid = (B, S // tm)
    row = lambda w: pl.BlockSpec((None, tm, w), lambda b, i: (b, i, 0))
    tab = pl.BlockSpec((tm, LANES), lambda b, i: (i, 0))
    vec = lambda w: _resident((1, w))
    out_w = (NA_WIDTH, NA_WIDTH, NA_WIDTH, MLA_HEADS * LANES, MLA_HEADS * LANES,
             MLA_HEADS * LANES, MLA_WIDTH, LANES)
    return pl.pallas_call(
        _proj_body,
        grid=grid,
        in_specs=[row(D), vec(D), _resident((D, P_PAD)), _resident(p['wq'].shape),
                  _resident(p['wkv'].shape), vec(LANES), vec(LANES), vec(Q_LORA), vec(KV_LORA),
                  vec(LANES), vec(LANES), vec(LANES), vec(LANES), tab, tab],
        out_specs=[row(w) for w in out_w],
        out_shape=[jax.ShapeDtypeStruct((B, S, w), BF16) for w in out_w],
        compiler_params=_cparams(("parallel", "arbitrary")),
        name="in_proj",
    )(x, p['g1'], p['win'], p['wq'], p['wkv'], p['gq'], p['gk'], p['gqa'], p['gkva'],
      p['gkr'], p['gqn'], p['gqr'], p['gkn'], p['cos'], p['sin'])


NA_QROWS = 4
NA_KROWS = NA_QROWS + NA_WIN_H


def _na_window_start(i, rows):
    return int(np.clip(NA_QROWS * i - NA_WIN_H // 2, 0, rows - NA_KROWS))


def _na_build_bias(tab_ref, bias_ref, rows):
    kh = min(NA_WIN_H, rows)
    shape = (GRID_W, LANES)
    c = lax.broadcasted_iota(jnp.int32, shape, 0)
    l = lax.broadcasted_iota(jnp.int32, shape, 1)
    cs = jnp.clip(c - NA_WIN_W // 2, 0, GRID_W - NA_WIN_W)
    in_lo = (l >= cs) & (l < cs + NA_WIN_W)
    in_hi = (l - GRID_W >= cs) & (l - GRID_W < cs + NA_WIN_W)
    neg = jnp.full(shape, NEG, F32)
    lo, hi = [], []
    for d in range(2 * NA_WIN_H - 1):
        base = jnp.broadcast_to(tab_ref[d:d + 1, :], shape)
        lo.append(jnp.where(in_lo, pltpu.roll(base, 0, 1, stride=1, stride_axis=0), NEG))
        hi.append(jnp.where(in_hi, pltpu.roll(base, GRID_W, 1, stride=1, stride_axis=0), NEG))
    for i in range(rows // NA_QROWS):
        ws = _na_window_start(i, rows)
        for qr in range(NA_QROWS):
            r = NA_QROWS * i + qr
            rs = int(np.clip(r - kh // 2, 0, rows - kh))
            for m in range(NA_KROWS // 2):
                k0 = ws + 2 * m
                a = lo[k0 - r + NA_WIN_H - 1] if rs <= k0 < rs + kh else neg
                b = hi[k0 + 1 - r + NA_WIN_H - 1] if rs <= k0 + 1 < rs + kh else neg
                bias_ref[i, qr * GRID_W:(qr + 1) * GRID_W, m * LANES:(m + 1) * LANES] = (
                    jnp.where(l < GRID_W, a, b))


def _na_body(q_ref, k_ref, v_ref, tab_ref, o_ref, bias_ref, vone_ref, *, rows):
    tq = NA_QROWS * GRID_W
    tk = NA_KROWS * GRID_W

    @pl.when(pl.program_id(1) == 0)
    def _():
        _na_build_bias(tab_ref, bias_ref, rows)

    vone_ref[:, :LANES] = v_ref[...]
    vone_ref[:, LANES:] = jnp.ones(v_ref.shape, BF16)

    for i in range(rows // NA_QROWS):
        ws = _na_window_start(i, rows) * GRID_W
        k = k_ref[ws:ws + tk, :]
        v_ones = vone_ref[ws:ws + tk, :]
        for c in range(tq // ATTN_SUB):
            lo = i * tq + c * ATTN_SUB
            s = _dot_t(q_ref[lo:lo + ATTN_SUB, :], k) + bias_ref[i, c * ATTN_SUB:(c + 1) * ATTN_SUB, :]
            o_ref[lo:lo + ATTN_SUB, :] = _softmax_pv(s, v_ones).astype(BF16)


def _na_bias_table(rpb):
    H, nd, nc = rpb.shape
    half = nc // 2
    fill = jnp.full((H, nd, LANES - nc), NEG, rpb.dtype)
    tab = jnp.concatenate([rpb[:, :, half:], fill, rpb[:, :, :half]], axis=2)
    return jnp.concatenate([tab, jnp.full((H, 16 - nd, LANES), NEG, rpb.dtype)], axis=1)


def _na_call(q, k, v, tab):
    B, S, _ = q.shape
    rows = S // GRID_W
    hb = lambda: pl.BlockSpec((None, S, LANES), lambda h, b: (b, 0, h))
    return pl.pallas_call(
        functools.partial(_na_body, rows=rows),
        grid=(NA_HEADS, B),
        in_specs=[hb(), hb(), hb(),
                  pl.BlockSpec((None,) + tab.shape[1:], lambda h, b: (h, 0, 0))],
        out_specs=hb(),
        out_shape=jax.ShapeDtypeStruct((B, S, NA_WIDTH), BF16),
        scratch_shapes=[pltpu.VMEM((rows // NA_QROWS, NA_QROWS * GRID_W, NA_KROWS * GRID_W), F32),
                        pltpu.VMEM((S, 2 * LANES), BF16)],
        compiler_params=_cparams(("arbitrary", "arbitrary")),
        name="na_attn",
    )(q, k, v, tab)


ATTN_SUB = 256


def _softmax_pv(s, v_ones):
    m = jnp.max(s, axis=-1, keepdims=True)
    e = jnp.exp2(s - m).astype(BF16)
    ol = jnp.dot(e, v_ones, preferred_element_type=F32)
    return ol[:, :LANES] / ol[:, LANES:]


def _mla_body(qn_ref, qpe_ref, kn_ref, kpe_ref, v_ref, o_ref, kcat_ref, vone_ref):
    @pl.when(pl.program_id(2) == 0)
    def _():
        kcat_ref[:, :LANES] = kn_ref[...]
        kcat_ref[:, LANES:] = kpe_ref[...]
        vone_ref[:, :LANES] = v_ref[...]
        vone_ref[:, LANES:] = jnp.ones(v_ref.shape, BF16)

    for c in range(qn_ref.shape[0] // ATTN_SUB):
        rows = slice(c * ATTN_SUB, (c + 1) * ATTN_SUB)
        q = jnp.concatenate([qn_ref[rows, :], qpe_ref[rows, :]], axis=1)
        s = _dot_t(q, kcat_ref[...])
        o_ref[rows, :] = _softmax_pv(s, vone_ref[...]).astype(BF16)


def _mla_call(qn, qpe, kn, kpe, v, tq):
    B, S, _ = qn.shape
    qb = lambda: pl.BlockSpec((None, tq, LANES), lambda b, h, i: (b, i, h))
    kb = lambda: pl.BlockSpec((None, S, LANES), lambda b, h, i: (b, 0, h))
    return pl.pallas_call(
        _mla_body,
        grid=(B, MLA_HEADS, S // tq),
        in_specs=[qb(), qb(), kb(), pl.BlockSpec((None, S, LANES), lambda b, h, i: (b, 0, 0)), kb()],
        out_specs=qb(),
        out_shape=jax.ShapeDtypeStruct((B, S, MLA_WIDTH), BF16),
        scratch_shapes=[pltpu.VMEM((S, 2 * LANES), BF16), pltpu.VMEM((S, 2 * LANES), BF16)],
        compiler_params=_cparams(("parallel", "arbitrary", "arbitrary")),
        name="mla_attn",
    )(qn, qpe, kn, kpe, v)


TOK_ROWS = D_MODEL // LANES
TOK_PITCH = TOK_ROWS + 1


PROJ_SUB = 256


def _store_token_major(ref, val, first_token=0):
    t = val.shape[0]
    base = first_token * TOK_PITCH
    for a in range(TOK_ROWS):
        ref[pl.ds(base + a, t, stride=TOK_PITCH), :] = val[:, a * LANES:(a + 1) * LANES]
    ref[pl.ds(base + TOK_ROWS, t, stride=TOK_PITCH), :] = jnp.zeros((t, LANES), val.dtype)


def _load_token_major(ref, t):
    return [ref[pl.ds(a, t, stride=TOK_PITCH), :] for a in range(TOK_ROWS)]


def _out_body(na_ref, mla_ref, x_ref, gna_ref, gmla_ref, wout_ref, g2_ref, wr_ref,
              h_ref, hn_ref, aff_ref):
    for c in range(x_ref.shape[0] // PROJ_SUB):
        rows = slice(c * PROJ_SUB, (c + 1) * PROJ_SUB)
        a = na_ref[rows, :].astype(F32)
        an = (a * _rms(a, NA_WIDTH) * gna_ref[...]).astype(BF16)
        b = mla_ref[rows, :].astype(F32)
        bn = (b * _rms(b, MLA_WIDTH) * gmla_ref[...]).astype(BF16)
        h = x_ref[rows, :]
        h = h + jnp.dot(an, wout_ref[:NA_WIDTH, :], preferred_element_type=F32)
        h = h + jnp.dot(bn, wout_ref[NA_WIDTH:, :], preferred_element_type=F32)
        hn = h * _rms(h, D_MODEL) * g2_ref[...]
        _store_token_major(h_ref, h, c * PROJ_SUB)
        _store_token_major(hn_ref, hn, c * PROJ_SUB)
        logits = jnp.dot(hn.astype(BF16), wr_ref[...], preferred_element_type=F32)
        lane = lax.broadcasted_iota(jnp.int32, logits.shape, 1)
        logits = jnp.where(lane < N_EXPERTS, logits, NEG)
        m = jnp.max(logits, axis=-1, keepdims=True)
        e = jnp.exp(logits - m)
        aff = e / jnp.sum(e, axis=-1, keepdims=True)
        aff_ref[:, rows] = aff.T[:N_EXPERTS, :]


def _out_call(na, mla, x, p, tm):
    B, S, D = x.shape
    row = lambda w: pl.BlockSpec((None, tm, w), lambda b, i: (b, i, 0))
    nblk = S // tm
    tokmaj = lambda: pl.BlockSpec((tm * TOK_PITCH, LANES), lambda b, i: (b * nblk + i, 0))
    return pl.pallas_call(
        _out_body,
        grid=(B, nblk),
        in_specs=[row(NA_WIDTH), row(MLA_WIDTH), row(D), _resident((1, NA_WIDTH)),
                  _resident((1, MLA_WIDTH)), _resident((NA_WIDTH + MLA_WIDTH, D)),
                  _resident((1, D)), _resident((D, LANES))],
        out_specs=[tokmaj(), tokmaj(), pl.BlockSpec((N_EXPERTS, tm), lambda b, i: (0, b * nblk + i))],
        out_shape=[jax.ShapeDtypeStruct((B * S * TOK_PITCH, LANES), F32),
                   jax.ShapeDtypeStruct((B * S * TOK_PITCH, LANES), F32),
                   jax.ShapeDtypeStruct((N_EXPERTS, B * S), F32)],
        compiler_params=_cparams(("parallel", "arbitrary")),
        name="out_proj",
    )(na, mla, x, p['gna'], p['gmla'], p['wout'], p['g2'], p['wr'])


def _tri(n, strict, upper):
    r = lax.broadcasted_iota(jnp.int32, (n, n), 0)
    c = lax.broadcasted_iota(jnp.int32, (n, n), 1)
    if upper:
        m = (r < c) if strict else (r <= c)
    else:
        m = (c < r) if strict else (c <= r)
    return jnp.where(m, 1.0, 0.0).astype(BF16)


def _dot_t(a, b):
    return lax.dot_general(a, b, (((1,), (1,)), ((), ())), preferred_element_type=F32)


def _topk_body(aff_ref, idx_ref, gate_ref, sel_ref, cum_ref, *, nb, cap):
    E = N_EXPERTS
    aff = aff_ref[...]

    def count(mask):
        c = jnp.sum(jnp.where(mask, 1.0, 0.0), axis=1, keepdims=True)
        return jnp.sum(c, axis=2, keepdims=True)

    def search(b, t):
        cand = t | jnp.left_shift(jnp.int32(1), 30 - b)
        return jnp.where(count(aff >= pltpu.bitcast(cand, F32)) >= cap, cand, t)

    thr = pltpu.bitcast(lax.fori_loop(0, 31, search, jnp.zeros((E, 1, 1), jnp.int32)), F32)

    upper_incl = _tri(LANES, strict=False, upper=True)
    lower_strict = jnp.broadcast_to(_tri(nb, strict=True, upper=False)[None], (E, nb, nb))

    def cumsum_incl(m):
        mb = m.astype(BF16)
        within = jnp.dot(mb.reshape(E * nb, LANES), upper_incl,
                         preferred_element_type=F32).reshape(E, nb, LANES)
        before = lax.dot_general(lower_strict, mb, (((2,), (1,)), ((0,), (0,))),
                                 preferred_element_type=F32)
        return within + jnp.sum(before, axis=2, keepdims=True)

    gt = aff > thr
    eq = jnp.where(aff == thr, 1.0, 0.0)
    need = cap - count(gt)
    eq_before = cumsum_incl(eq) - eq
    sel = jnp.where(gt | ((eq > 0.5) & (eq_before < need)), 1.0, 0.0)
    sel_ref[...] = sel
    cum_ref[...] = cumsum_incl(sel)

    ones_rows = jnp.ones((8, LANES), BF16)
    lane0 = jnp.where(lax.broadcasted_iota(jnp.int32, (8, LANES), 1) == 0, 1.0, 0.0).astype(BF16)
    ones_nb = jnp.ones((nb, LANES), BF16)
    ones_sq = jnp.ones((LANES, LANES), BF16)
    upper_strict_nb = _tri(nb, strict=True, upper=True)
    slot_nb = lax.broadcasted_iota(jnp.int32, (cap, nb), 0).astype(F32)
    slot = lax.broadcasted_iota(jnp.int32, (cap, LANES), 0).astype(F32)
    lane = lax.broadcasted_iota(jnp.int32, (cap, LANES), 1).astype(F32)

    def compact(e, carry):
        m = sel_ref[e].astype(BF16)
        c = cum_ref[e]
        row_tot = _dot_t(ones_rows, m)
        row_off = jnp.dot(row_tot.astype(BF16), upper_strict_nb, preferred_element_type=F32)
        row_end = row_off + row_tot
        off_b = row_off[0:1, :]
        end_b = row_end[0:1, :]
        onehot = jnp.where((off_b <= slot_nb) & (slot_nb < end_b), 1.0, 0.0).astype(BF16)
        rows_before = jnp.where(end_b <= slot_nb, 1.0, 0.0).astype(BF16)
        row_idx = jnp.dot(rows_before, ones_nb, preferred_element_type=F32)
        c_hi = jnp.floor(c * (1.0 / LANES))
        c_lo = c - c_hi * LANES
        c_row = (jnp.dot(onehot, c_hi.astype(BF16), preferred_element_type=F32) * LANES
                 + jnp.dot(onehot, c_lo.astype(BF16), preferred_element_type=F32))
        before = jnp.where(c_row <= slot, 1.0, 0.0).astype(BF16)
        lane_idx = jnp.dot(before, ones_sq, preferred_element_type=F32)
        tok = (_dot_t(lane0, row_idx.astype(BF16)) * LANES
               + _dot_t(lane0, lane_idx.astype(BF16)))
        idx_ref[e] = tok.astype(jnp.int32)
        a = aff_ref[e]
        a1 = a.astype(BF16)
        r1 = a - a1.astype(F32)
        a2 = r1.astype(BF16)
        a3 = (r1 - a2.astype(F32)).astype(BF16)
        a_row = (jnp.dot(onehot, a1, preferred_element_type=F32)
                 + jnp.dot(onehot, a2, preferred_element_type=F32)
                 + jnp.dot(onehot, a3, preferred_element_type=F32))
        gate_ref[e] = jnp.sum(jnp.where(lane == lane_idx, a_row, 0.0), axis=1, keepdims=True)
        return carry

    lax.fori_loop(0, E, compact, 0)


def _topk_call(aff_t):
    E, N = aff_t.shape
    nb = N // LANES
    cap = EC_CAPACITY * N // N_EXPERTS
    idx, gates = pl.pallas_call(
        functools.partial(_topk_body, nb=nb, cap=cap),
        out_shape=[jax.ShapeDtypeStruct((E, 8, cap), jnp.int32),
                   jax.ShapeDtypeStruct((E, cap, 1), F32)],
        scratch_shapes=[pltpu.VMEM((E, nb, LANES), F32), pltpu.VMEM((E, nb, LANES), F32)],
        compiler_params=pltpu.CompilerParams(vmem_limit_bytes=VMEM_LIMIT),
        name="ec_topk",
    )(aff_t.reshape(E, nb, LANES))
    return idx[:, 0, :], gates


FFN_ROWS = 1024
FFN_FC = 256
FFN_DC = 512
FFN_UNROLL = 8


def _ffn_body(idx_ref, hn_hbm, yin_hbm, gate_ref, wg_ref, wu_ref, wd_ref, y_hbm,
              xrows_ref, yrows_ref, xe_ref, acc_ref, sem, *, cap, rows):
    del yin_hbm
    e, r, f = pl.program_id(0), pl.program_id(1), pl.program_id(2)
    nr, nf = pl.num_programs(1), pl.num_programs(2)
    nblocks = pl.num_programs(0) * nr
    block = e * nr + r
    base = block * rows
    last_f = f == nf - 1

    def hbm_token(ref, slot):
        return ref.at[pl.ds(idx_ref[slot] * TOK_PITCH, TOK_ROWS)]

    def vmem_token(ref, s):
        return ref.at[pl.ds(s * TOK_PITCH, TOK_ROWS)]

    def x_gather(first_slot, s):
        return pltpu.make_async_copy(hbm_token(hn_hbm, first_slot + s), vmem_token(xrows_ref, s), sem.at[0])

    def y_gather(s):
        return pltpu.make_async_copy(hbm_token(y_hbm, base + s), vmem_token(yrows_ref, s), sem.at[1])

    def scatter(first_slot, s):
        return pltpu.make_async_copy(vmem_token(yrows_ref, s), hbm_token(y_hbm, first_slot + s), sem.at[2])

    def for_rows(fn):
        def body(k, c):
            for j in range(FFN_UNROLL):
                fn(k * FFN_UNROLL + j, j)
            return c
        lax.fori_loop(0, rows // FFN_UNROLL, body, 0)

    @pl.when((f == 0) & (block == 0))
    def _():
        for_rows(lambda s, j: x_gather(base, s).start(priority=1))

    @pl.when(f == 0)
    def _():
        for_rows(lambda s, j: x_gather(base, s).wait())
        for a, col in enumerate(_load_token_major(xrows_ref, rows)):
            xe_ref[:, a * LANES:(a + 1) * LANES] = col.astype(BF16)

    @pl.when((f == 1) & (block > 0))
    def _():
        for_rows(lambda s, j: scatter(base - rows, s).wait())

    @pl.when(f == 1)
    def _():
        for_rows(lambda s, j: y_gather(s).start(priority=1))

    @pl.when((f == 1) & (block + 1 < nblocks))
    def _():
        for_rows(lambda s, j: x_gather(base + rows, s).start(priority=1))

    x = xe_ref[...]
    g = jnp.dot(x, wg_ref[...].astype(BF16), preferred_element_type=F32)
    u = jnp.dot(x, wu_ref[...].astype(BF16), preferred_element_type=F32)
    hid = (jax.nn.silu(g) * u).astype(BF16)
    wd = wd_ref[...].astype(BF16)

    def down(first, last):
        gate = gate_ref[...]
        for c in range(D_MODEL // FFN_DC):
            cs = slice(c * FFN_DC, (c + 1) * FFN_DC)
            part = jnp.dot(hid, wd[:, cs], preferred_element_type=F32)
            total = part if first else acc_ref[:, cs] + part
            if not last:
                acc_ref[:, cs] = total
                continue
            for a in range(c * FFN_DC // LANES, (c + 1) * FFN_DC // LANES):
                group = pl.ds(a, rows, stride=TOK_PITCH)
                lanes = slice(a * LANES - c * FFN_DC, (a + 1) * LANES - c * FFN_DC)
                yrows_ref[group, :] = yrows_ref[group, :] + total[:, lanes] * gate

    pl.when(f == 0)(lambda: down(True, False))
    pl.when((f != 0) & jnp.logical_not(last_f))(lambda: down(False, False))

    @pl.when(last_f)
    def _():
        for_rows(lambda s, j: y_gather(s).wait())
        down(False, True)
        for_rows(lambda s, j: scatter(base, s).start(priority=j % 2))

    @pl.when(last_f & (block == nblocks - 1))
    def _():
        for_rows(lambda s, j: scatter(base, s).wait())


def _ffn_call(idx, gates, hn, h, w_gate, w_up, w_down):
    E, cap = idx.shape
    D = D_MODEL
    F = w_gate.shape[2]
    rows = min(FFN_ROWS, cap)
    tokbuf = pltpu.VMEM((rows * TOK_PITCH, LANES), F32)
    grid_spec = pltpu.PrefetchScalarGridSpec(
        num_scalar_prefetch=1,
        grid=(E, cap // rows, F // FFN_FC),
        in_specs=[
            pl.BlockSpec(memory_space=pl.ANY),
            pl.BlockSpec(memory_space=pl.ANY),
            pl.BlockSpec((None, rows, 1), lambda e, r, f, idx: (e, r, 0)),
            pl.BlockSpec((None, D, FFN_FC), lambda e, r, f, idx: (e, 0, f)),
            pl.BlockSpec((None, D, FFN_FC), lambda e, r, f, idx: (e, 0, f)),
            pl.BlockSpec((None, FFN_FC, D), lambda e, r, f, idx: (e, f, 0)),
        ],
        out_specs=pl.BlockSpec(memory_space=pl.ANY),
        scratch_shapes=[tokbuf, tokbuf, pltpu.VMEM((rows, D), BF16), pltpu.VMEM((rows, D), F32),
                        pltpu.SemaphoreType.DMA((3,))],
    )
    assert F // FFN_FC >= 2, "the next block's row gather is issued from hidden chunk 1"
    return pl.pallas_call(
        functools.partial(_ffn_body, cap=cap, rows=rows),
        grid_spec=grid_spec,
        out_shape=jax.ShapeDtypeStruct(h.shape, F32),
        input_output_aliases={2: 0},
        compiler_params=_cparams(("arbitrary", "arbitrary", "arbitrary")),
        name="ec_ffn",
    )(idx.reshape(-1), hn, h, gates, w_gate, w_up, w_down)


def _row_major_body(y_ref, o_ref):
    for a, col in enumerate(_load_token_major(y_ref, o_ref.shape[0])):
        o_ref[:, a * LANES:(a + 1) * LANES] = col


def _row_major_call(y, B, S, tm):
    nblk = S // tm
    return pl.pallas_call(
        _row_major_body,
        grid=(B, nblk),
        in_specs=[pl.BlockSpec((tm * TOK_PITCH, LANES), lambda b, i: (b * nblk + i, 0))],
        out_specs=pl.BlockSpec((None, tm, D_MODEL), lambda b, i: (b, i, 0)),
        out_shape=jax.ShapeDtypeStruct((B, S, D_MODEL), F32),
        compiler_params=_cparams(("parallel", "arbitrary")),
        name="row_major",
    )(y)


def _rope_tables(S):
    t = np.arange(S)
    row = (t // GRID_W).astype(np.float32)
    col = (t % GRID_W).astype(np.float32)
    nf = ROPE_DIM // 4
    inv = (1.0 / (np.float32(ROPE_BASE) ** (np.arange(nf, dtype=np.float32) / nf))).astype(np.float32)
    ar = (row[:, None] * inv).astype(np.float64)
    ac = (col[:, None] * inv).astype(np.float64)
    cr, sr, cc, sc = np.cos(ar), np.sin(ar), np.cos(ac), np.sin(ac)
    z = np.zeros((S, LANES - ROPE_DIM))
    cos = np.concatenate([cr, cr, cc, cc, z], axis=1).astype(np.float32)
    sin = np.concatenate([-sr, sr, -sc, sc, z], axis=1).astype(np.float32)
    return jnp.asarray(cos), jnp.asarray(sin)


def _pad_lanes(v, fill=0.0):
    return jnp.concatenate([v, jnp.full((LANES - v.shape[0],), fill, v.dtype)])[None, :]


def _prepare(S, norm1_g, w_in, na_q_norm, na_k_norm, na_rpb, q_a_norm, w_q_up, kv_a_norm, w_kv_up,
             mla_qn_norm, mla_qr_norm, mla_kn_norm, mla_kr_norm, na_out_norm, mla_out_norm, w_out,
             norm2_g, w_router):
    p = {}
    p['g1'] = norm1_g[None, :]
    p['win'] = jnp.concatenate(
        [w_in, jnp.zeros((D_MODEL, ROPE_PAD - ROPE_DIM), w_in.dtype)], axis=1).astype(BF16)
    wq = w_q_up.reshape(Q_LORA, MLA_HEADS, QK_DIM)
    wq_nope = wq[:, :, :NOPE_DIM].reshape(Q_LORA, MLA_HEADS * NOPE_DIM)
    wq_pe = jnp.pad(wq[:, :, NOPE_DIM:], ((0, 0), (0, 0), (0, LANES - ROPE_DIM)))
    p['wq'] = jnp.concatenate([wq_nope, wq_pe.reshape(Q_LORA, MLA_HEADS * LANES)], axis=1).astype(BF16)
    wkv = w_kv_up.reshape(KV_LORA, MLA_HEADS, NOPE_DIM + V_DIM)
    p['wkv'] = jnp.concatenate([wkv[:, :, :NOPE_DIM].reshape(KV_LORA, -1),
                                wkv[:, :, NOPE_DIM:].reshape(KV_LORA, -1)], axis=1).astype(BF16)
    p['gq'] = (na_q_norm * (NA_HEAD_DIM ** -0.5 * LOG2E))[None, :]
    p['gk'] = na_k_norm[None, :]
    p['gqa'] = q_a_norm[None, :]
    p['gkva'] = kv_a_norm[None, :]
    p['gkr'] = _pad_lanes(mla_kr_norm)
    p['gqn'] = (mla_qn_norm * (QK_DIM ** -0.5 * LOG2E))[None, :]
    p['gqr'] = _pad_lanes(mla_qr_norm * (QK_DIM ** -0.5 * LOG2E))
    p['gkn'] = mla_kn_norm[None, :]
    p['cos'], p['sin'] = _rope_tables(S)
    p['bias'] = _na_bias_table(na_rpb * LOG2E)
    p['gna'] = na_out_norm[None, :]
    p['gmla'] = mla_out_norm[None, :]
    p['wout'] = w_out.astype(BF16)
    p['g2'] = norm2_g[None, :]
    p['wr'] = jnp.pad(w_router, ((0, 0), (0, LANES - N_EXPERTS))).astype(BF16)
    return p


def _attention_block(x, p):
    qna, kna, vna, qn, qpe, kn, vm, kpe = _proj_call(x, p, tm=512)
    out_na = _na_call(qna, kna, vna, p['bias'])
    out_mla = _mla_call(qn, qpe, kn, kpe, vm, tq=x.shape[1])
    return _out_call(out_na, out_mla, x, p, tm=512)


def _layer(x, p, w_gate, w_up, w_down):
    B, S, D = x.shape
    h, hn, aff_t = _attention_block(x, p)
    idx, gates = _topk_call(aff_t)
    y = _ffn_call(idx, gates, hn, h, w_gate, w_up, w_down)
    return _row_major_call(y, B, S, tm=512)


def kernel(x_prompt, x_sample, norm1_g, w_in, na_q_norm, na_k_norm, na_rpb, q_a_norm, w_q_up, kv_a_norm, w_kv_up, mla_qn_norm, mla_qr_norm, mla_kn_norm, mla_kr_norm, na_out_norm, mla_out_norm, w_out, norm2_g, w_router, w_gate, w_up, w_down):
    hp, hs = x_prompt, x_sample
    for l in range(norm1_g.shape[0]):
        p = _prepare(hp.shape[1], norm1_g[l], w_in[l], na_q_norm[l], na_k_norm[l], na_rpb[l],
                     q_a_norm[l], w_q_up[l], kv_a_norm[l], w_kv_up[l], mla_qn_norm[l],
                     mla_qr_norm[l], mla_kn_norm[l], mla_kr_norm[l], na_out_norm[l],
                     mla_out_norm[l], w_out[l], norm2_g[l], w_router[l])
        hp = _layer(hp, p, w_gate[l], w_up[l], w_down[l])
        hs = _layer(hs, p, w_gate[l], w_up[l], w_down[l])
    return (hp, hs)
```

```python
import functools

import jax
import jax.numpy as jnp
import numpy as np
from jax import lax
from jax.experimental import pallas as pl
from jax.experimental.pallas import tpu as pltpu

D_MODEL = 2048
GRID_W = 64
NA_HEADS = 8
NA_HEAD_DIM = 128
NA_WIN_H = 8
NA_WIN_W = 16
MLA_HEADS = 8
Q_LORA = 512
KV_LORA = 512
NOPE_DIM = 128
ROPE_DIM = 64
V_DIM = 128
ROPE_BASE = 10000.0
N_EXPERTS = 16
EC_CAPACITY = 2
EXPERT_FF = 1024
EPS = 1e-6

NA_WIDTH = NA_HEADS * NA_HEAD_DIM
MLA_WIDTH = MLA_HEADS * V_DIM
QK_DIM = NOPE_DIM + ROPE_DIM
LANES = 128
ROPE_PAD = LANES
P_PAD = 3 * NA_WIDTH + Q_LORA + KV_LORA + ROPE_PAD
NEG = -1e30
LOG2E = 1.4426950408889634

BF16 = jnp.bfloat16
F32 = jnp.float32

VMEM_LIMIT = 56 * 1024 * 1024


def _cparams(sem):
    return pltpu.CompilerParams(dimension_semantics=sem, vmem_limit_bytes=VMEM_LIMIT)


def _resident(shape):
    nd = len(shape)
    return pl.BlockSpec(shape, lambda *_: (0,) * nd, pipeline_mode=pl.Buffered(1))


def _rms(a, width):
    return lax.rsqrt(jnp.sum(a * a, axis=-1, keepdims=True) * (1.0 / width) + EPS)


def _rope(a, cos, sin):
    lane = lax.broadcasted_iota(jnp.int32, a.shape, 1)
    up = pltpu.roll(a, LANES - 16, axis=1)
    dn = pltpu.roll(a, 16, axis=1)
    partner = jnp.where((lane % 32) < 16, up, dn)
    return a * cos + partner * sin


def _proj_body(x_ref, g1_ref, win_ref, wq_ref, wkv_ref, gq_ref, gk_ref, gqa_ref, gkva_ref,
               gkr_ref, gqn_ref, gqr_ref, gkn_ref, cos_ref, sin_ref,
               qna_ref, kna_ref, vna_ref, qn_ref, qpe_ref, kn_ref, vm_ref, kpe_ref):
    half = NA_WIDTH // 2
    nh = MLA_HEADS * LANES

    def chain(rows):
        x = x_ref[rows, :]
        xn = (x * _rms(x, D_MODEL) * g1_ref[...]).astype(BF16)
        cos = cos_ref[rows, :]
        sin = sin_ref[rows, :]

        def proj(lo, hi):
            return jnp.dot(xn, win_ref[:, lo:hi], preferred_element_type=F32)

        def heads_to(ref, a, g, width, base, rope=False):
            for h in range(a.shape[1] // LANES):
                ah = a[:, h * LANES:(h + 1) * LANES]
                y = ah * _rms(ah, width) * g
                if rope:
                    y = _rope(y, cos, sin)
                ref[rows, base + h * LANES: base + (h + 1) * LANES] = y.astype(BF16)

        for c in range(2):
            heads_to(qna_ref, proj(c * half, (c + 1) * half), gq_ref[...], NA_HEAD_DIM, c * half)
        for c in range(2):
            heads_to(kna_ref, proj(NA_WIDTH + c * half, NA_WIDTH + (c + 1) * half), gk_ref[...],
                     NA_HEAD_DIM, c * half)
        for c in range(2):
            vna_ref[rows, c * half:(c + 1) * half] = proj(2 * NA_WIDTH + c * half,
                                                          2 * NA_WIDTH + (c + 1) * half).astype(BF16)

        o = 3 * NA_WIDTH
        ql = proj(o, o + Q_LORA)
        qln = (ql * _rms(ql, Q_LORA) * gqa_ref[...]).astype(BF16)
        kvl = proj(o + Q_LORA, o + Q_LORA + KV_LORA)
        kvln = (kvl * _rms(kvl, KV_LORA) * gkva_ref[...]).astype(BF16)
        kp = proj(o + Q_LORA + KV_LORA, P_PAD)
        heads_to(kpe_ref, kp, gkr_ref[...], ROPE_DIM, 0, rope=True)

        for c in range(2):
            a = jnp.dot(qln, wq_ref[:, c * half:(c + 1) * half], preferred_element_type=F32)
            heads_to(qn_ref, a, gqn_ref[...], NOPE_DIM, c * half)
        for c in range(2):
            a = jnp.dot(qln, wq_ref[:, nh + c * half: nh + (c + 1) * half], preferred_element_type=F32)
            heads_to(qpe_ref, a, gqr_ref[...], ROPE_DIM, c * half, rope=True)
        for c in range(2):
            a = jnp.dot(kvln, wkv_ref[:, c * half:(c + 1) * half], preferred_element_type=F32)
            heads_to(kn_ref, a, gkn_ref[...], NOPE_DIM, c * half)
        for c in range(2):
            vm_ref[rows, c * half:(c + 1) * half] = jnp.dot(
                kvln, wkv_ref[:, nh + c * half: nh + (c + 1) * half],
                preferred_element_type=F32).astype(BF16)

    for c in range(x_ref.shape[0] // PROJ_SUB):
        chain(slice(c * PROJ_SUB, (c + 1) * PROJ_SUB))


def _proj_call(x, p, tm):
    B, S, D = x.shape
    grid = (B, S // tm)
    row = lambda w: pl.BlockSpec((None, tm, w), lambda b, i: (b, i, 0))
    tab = pl.BlockSpec((tm, LANES), lambda b, i: (i, 0))
    vec = lambda w: _resident((1, w))
    out_w = (NA_WIDTH, NA_WIDTH, NA_WIDTH, MLA_HEADS * LANES, MLA_HEADS * LANES,
             MLA_HEADS * LANES, MLA_WIDTH, LANES)
    return pl.pallas_call(
        _proj_body,
        grid=grid,
        in_specs=[row(D), vec(D), _resident((D, P_PAD)), _resident(p['wq'].shape),
                  _resident(p['wkv'].shape), vec(LANES), vec(LANES), vec(Q_LORA), vec(KV_LORA),
                  vec(LANES), vec(LANES), vec(LANES), vec(LANES), tab, tab],
        out_specs=[row(w) for w in out_w],
        out_shape=[jax.ShapeDtypeStruct((B, S, w), BF16) for w in out_w],
        compiler_params=_cparams(("parallel", "arbitrary")),
        name="in_proj",
    )(x, p['g1'], p['win'], p['wq'], p['wkv'], p['gq'], p['gk'], p['gqa'], p['gkva'],
      p['gkr'], p['gqn'], p['gqr'], p['gkn'], p['cos'], p['sin'])


NA_QROWS = 4
NA_KROWS = NA_QROWS + NA_WIN_H


def _na_window_start(i, rows):
    return int(np.clip(NA_QROWS * i - NA_WIN_H // 2, 0, rows - NA_KROWS))


def _na_build_bias(tab_ref, bias_ref, rows):
    kh = min(NA_WIN_H, rows)
    shape = (GRID_W, LANES)
    c = lax.broadcasted_iota(jnp.int32, shape, 0)
    l = lax.broadcasted_iota(jnp.int32, shape, 1)
    cs = jnp.clip(c - NA_WIN_W // 2, 0, GRID_W - NA_WIN_W)
    in_lo = (l >= cs) & (l < cs + NA_WIN_W)
    in_hi = (l - GRID_W >= cs) & (l - GRID_W < cs + NA_WIN_W)
    neg = jnp.full(shape, NEG, F32)
    lo, hi = [], []
    for d in range(2 * NA_WIN_H - 1):
        base = jnp.broadcast_to(tab_ref[d:d + 1, :], shape)
        lo.append(jnp.where(in_lo, pltpu.roll(base, 0, 1, stride=1, stride_axis=0), NEG))
        hi.append(jnp.where(in_hi, pltpu.roll(base, GRID_W, 1, stride=1, stride_axis=0), NEG))
    for i in range(rows // NA_QROWS):
        ws = _na_window_start(i, rows)
        for qr in range(NA_QROWS):
            r = NA_QROWS * i + qr
            rs = int(np.clip(r - kh // 2, 0, rows - kh))
            for m in range(NA_KROWS // 2):
                k0 = ws + 2 * m
                a = lo[k0 - r + NA_WIN_H - 1] if rs <= k0 < rs + kh else neg
                b = hi[k0 + 1 - r + NA_WIN_H - 1] if rs <= k0 + 1 < rs + kh else neg
                bias_ref[i, qr * GRID_W:(qr + 1) * GRID_W, m * LANES:(m + 1) * LANES] = (
                    jnp.where(l < GRID_W, a, b))


def _na_body(q_ref, k_ref, v_ref, tab_ref, o_ref, bias_ref, vone_ref, *, rows):
    tq = NA_QROWS * GRID_W
    tk = NA_KROWS * GRID_W

    @pl.when(pl.program_id(1) == 0)
    def _():
        _na_build_bias(tab_ref, bias_ref, rows)

    vone_ref[:, :LANES] = v_ref[...]
    vone_ref[:, LANES:] = jnp.ones(v_ref.shape, BF16)

    for i in range(rows // NA_QROWS):
        ws = _na_window_start(i, rows) * GRID_W
        k = k_ref[ws:ws + tk, :]
        v_ones = vone_ref[ws:ws + tk, :]
        for c in range(tq // ATTN_SUB):
            lo = i * tq + c * ATTN_SUB
            s = _dot_t(q_ref[lo:lo + ATTN_SUB, :], k) + bias_ref[i, c * ATTN_SUB:(c + 1) * ATTN_SUB, :]
            o_ref[lo:lo + ATTN_SUB, :] = _softmax_pv(s, v_ones).astype(BF16)


def _na_bias_table(rpb):
    H, nd, nc = rpb.shape
    half = nc // 2
    fill = jnp.full((H, nd, LANES - nc), NEG, rpb.dtype)
    tab = jnp.concatenate([rpb[:, :, half:], fill, rpb[:, :, :half]], axis=2)
    return jnp.concatenate([tab, jnp.full((H, 16 - nd, LANES), NEG, rpb.dtype)], axis=1)


def _na_call(q, k, v, tab):
    B, S, _ = q.shape
    rows = S // GRID_W
    hb = lambda: pl.BlockSpec((None, S, LANES), lambda h, b: (b, 0, h))
    return pl.pallas_call(
        functools.partial(_na_body, rows=rows),
        grid=(NA_HEADS, B),
        in_specs=[hb(), hb(), hb(),
                  pl.BlockSpec((None,) + tab.shape[1:], lambda h, b: (h, 0, 0))],
        out_specs=hb(),
        out_shape=jax.ShapeDtypeStruct((B, S, NA_WIDTH), BF16),
        scratch_shapes=[pltpu.VMEM((rows // NA_QROWS, NA_QROWS * GRID_W, NA_KROWS * GRID_W), F32),
                        pltpu.VMEM((S, 2 * LANES), BF16)],
        compiler_params=_cparams(("arbitrary", "arbitrary")),
        name="na_attn",
    )(q, k, v, tab)


ATTN_SUB = 256


def _softmax_pv(s, v_ones):
    m = jnp.max(s, axis=-1, keepdims=True)
    e = jnp.exp2(s - m).astype(BF16)
    ol = jnp.dot(e, v_ones, preferred_element_type=F32)
    return ol[:, :LANES] / ol[:, LANES:]


def _mla_body(qn_ref, qpe_ref, kn_ref, kpe_ref, v_ref, o_ref, kcat_ref, vone_ref):
    @pl.when(pl.program_id(2) == 0)
    def _():
        kcat_ref[:, :LANES] = kn_ref[...]
        kcat_ref[:, LANES:] = kpe_ref[...]
        vone_ref[:, :LANES] = v_ref[...]
        vone_ref[:, LANES:] = jnp.ones(v_ref.shape, BF16)

    for c in range(qn_ref.shape[0] // ATTN_SUB):
        rows = slice(c * ATTN_SUB, (c + 1) * ATTN_SUB)
        q = jnp.concatenate([qn_ref[rows, :], qpe_ref[rows, :]], axis=1)
        s = _dot_t(q, kcat_ref[...])
        o_ref[rows, :] = _softmax_pv(s, vone_ref[...]).astype(BF16)


def _mla_call(qn, qpe, kn, kpe, v, tq):
    B, S, _ = qn.shape
    qb = lambda: pl.BlockSpec((None, tq, LANES), lambda b, h, i: (b, i, h))
    kb = lambda: pl.BlockSpec((None, S, LANES), lambda b, h, i: (b, 0, h))
    return pl.pallas_call(
        _mla_body,
        grid=(B, MLA_HEADS, S // tq),
        in_specs=[qb(), qb(), kb(), pl.BlockSpec((None, S, LANES), lambda b, h, i: (b, 0, 0)), kb()],
        out_specs=qb(),
        out_shape=jax.ShapeDtypeStruct((B, S, MLA_WIDTH), BF16),
        scratch_shapes=[pltpu.VMEM((S, 2 * LANES), BF16), pltpu.VMEM((S, 2 * LANES), BF16)],
        compiler_params=_cparams(("parallel", "arbitrary", "arbitrary")),
        name="mla_attn",
    )(qn, qpe, kn, kpe, v)


TOK_ROWS = D_MODEL // LANES
TOK_PITCH = TOK_ROWS + 1


PROJ_SUB = 256


def _store_token_major(ref, val, first_token=0):
    t = val.shape[0]
    base = first_token * TOK_PITCH
    for a in range(TOK_ROWS):
        ref[pl.ds(base + a, t, stride=TOK_PITCH), :] = val[:, a * LANES:(a + 1) * LANES]
    ref[pl.ds(base + TOK_ROWS, t, stride=TOK_PITCH), :] = jnp.zeros((t, LANES), val.dtype)


def _load_token_major(ref, t):
    return [ref[pl.ds(a, t, stride=TOK_PITCH), :] for a in range(TOK_ROWS)]


def _out_body(na_ref, mla_ref, x_ref, gna_ref, gmla_ref, wout_ref, g2_ref, wr_ref,
              h_ref, hn_ref, aff_ref):
    for c in range(x_ref.shape[0] // PROJ_SUB):
        rows = slice(c * PROJ_SUB, (c + 1) * PROJ_SUB)
        a = na_ref[rows, :].astype(F32)
        an = (a * _rms(a, NA_WIDTH) * gna_ref[...]).astype(BF16)
        b = mla_ref[rows, :].astype(F32)
        bn = (b * _rms(b, MLA_WIDTH) * gmla_ref[...]).astype(BF16)
        h = x_ref[rows, :]
        h = h + jnp.dot(an, wout_ref[:NA_WIDTH, :], preferred_element_type=F32)
        h = h + jnp.dot(bn, wout_ref[NA_WIDTH:, :], preferred_element_type=F32)
        hn = h * _rms(h, D_MODEL) * g2_ref[...]
        _store_token_major(h_ref, h, c * PROJ_SUB)
        _store_token_major(hn_ref, hn, c * PROJ_SUB)
        logits = jnp.dot(hn.astype(BF16), wr_ref[...], preferred_element_type=F32)
        lane = lax.broadcasted_iota(jnp.int32, logits.shape, 1)
        logits = jnp.where(lane < N_EXPERTS, logits, NEG)
        m = jnp.max(logits, axis=-1, keepdims=True)
        e = jnp.exp(logits - m)
        aff = e / jnp.sum(e, axis=-1, keepdims=True)
        aff_ref[:, rows] = aff.T[:N_EXPERTS, :]


def _out_call(na, mla, x, p, tm):
    B, S, D = x.shape
    row = lambda w: pl.BlockSpec((None, tm, w), lambda b, i: (b, i, 0))
    nblk = S // tm
    tokmaj = lambda: pl.BlockSpec((tm * TOK_PITCH, LANES), lambda b, i: (b * nblk + i, 0))
    return pl.pallas_call(
        _out_body,
        grid=(B, nblk),
        in_specs=[row(NA_WIDTH), row(MLA_WIDTH), row(D), _resident((1, NA_WIDTH)),
                  _resident((1, MLA_WIDTH)), _resident((NA_WIDTH + MLA_WIDTH, D)),
                  _resident((1, D)), _resident((D, LANES))],
        out_specs=[tokmaj(), tokmaj(), pl.BlockSpec((N_EXPERTS, tm), lambda b, i: (0, b * nblk + i))],
        out_shape=[jax.ShapeDtypeStruct((B * S * TOK_PITCH, LANES), F32),
                   jax.ShapeDtypeStruct((B * S * TOK_PITCH, LANES), F32),
                   jax.ShapeDtypeStruct((N_EXPERTS, B * S), F32)],
        compiler_params=_cparams(("parallel", "arbitrary")),
        name="out_proj",
    )(na, mla, x, p['gna'], p['gmla'], p['wout'], p['g2'], p['wr'])


def _tri(n, strict, upper):
    r = lax.broadcasted_iota(jnp.int32, (n, n), 0)
    c = lax.broadcasted_iota(jnp.int32, (n, n), 1)
    if upper:
        m = (r < c) if strict else (r <= c)
    else:
        m = (c < r) if strict else (c <= r)
    return jnp.where(m, 1.0, 0.0).astype(BF16)


def _dot_t(a, b):
    return lax.dot_general(a, b, (((1,), (1,)), ((), ())), preferred_element_type=F32)


def _topk_body(aff_ref, idx_ref, gate_ref, sel_ref, cum_ref, *, nb, cap):
    E = N_EXPERTS
    aff = aff_ref[...]

    def count(mask):
        c = jnp.sum(jnp.where(mask, 1.0, 0.0), axis=1, keepdims=True)
        return jnp.sum(c, axis=2, keepdims=True)

    def search(b, t):
        cand = t | jnp.left_shift(jnp.int32(1), 30 - b)
        return jnp.where(count(aff >= pltpu.bitcast(cand, F32)) >= cap, cand, t)

    thr = pltpu.bitcast(lax.fori_loop(0, 31, search, jnp.zeros((E, 1, 1), jnp.int32)), F32)

    upper_incl = _tri(LANES, strict=False, upper=True)
    lower_strict = jnp.broadcast_to(_tri(nb, strict=True, upper=False)[None], (E, nb, nb))

    def cumsum_incl(m):
        mb = m.astype(BF16)
        within = jnp.dot(mb.reshape(E * nb, LANES), upper_incl,
                         preferred_element_type=F32).reshape(E, nb, LANES)
        before = lax.dot_general(lower_strict, mb, (((2,), (1,)), ((0,), (0,))),
                                 preferred_element_type=F32)
        return within + jnp.sum(before, axis=2, keepdims=True)

    gt = aff > thr
    eq = jnp.where(aff == thr, 1.0, 0.0)
    need = cap - count(gt)
    eq_before = cumsum_incl(eq) - eq
    sel = jnp.where(gt | ((eq > 0.5) & (eq_before < need)), 1.0, 0.0)
    sel_ref[...] = sel
    cum_ref[...] = cumsum_incl(sel)

    ones_rows = jnp.ones((8, LANES), BF16)
    lane0 = jnp.where(lax.broadcasted_iota(jnp.int32, (8, LANES), 1) == 0, 1.0, 0.0).astype(BF16)
    ones_nb = jnp.ones((nb, LANES), BF16)
    ones_sq = jnp.ones((LANES, LANES), BF16)
    upper_strict_nb = _tri(nb, strict=True, upper=True)
    slot_nb = lax.broadcasted_iota(jnp.int32, (cap, nb), 0).astype(F32)
    slot = lax.broadcasted_iota(jnp.int32, (cap, LANES), 0).astype(F32)
    lane = lax.broadcasted_iota(jnp.int32, (cap, LANES), 1).astype(F32)

    def compact(e, carry):
        m = sel_ref[e].astype(BF16)
        c = cum_ref[e]
        row_tot = _dot_t(ones_rows, m)
        row_off = jnp.dot(row_tot.astype(BF16), upper_strict_nb, preferred_element_type=F32)
        row_end = row_off + row_tot
        off_b = row_off[0:1, :]
        end_b = row_end[0:1, :]
        onehot = jnp.where((off_b <= slot_nb) & (slot_nb < end_b), 1.0, 0.0).astype(BF16)
        rows_before = jnp.where(end_b <= slot_nb, 1.0, 0.0).astype(BF16)
        row_idx = jnp.dot(rows_before, ones_nb, preferred_element_type=F32)
        c_hi = jnp.floor(c * (1.0 / LANES))
        c_lo = c - c_hi * LANES
        c_row = (jnp.dot(onehot, c_hi.astype(BF16), preferred_element_type=F32) * LANES
                 + jnp.dot(onehot, c_lo.astype(BF16), preferred_element_type=F32))
        before = jnp.where(c_row <= slot, 1.0, 0.0).astype(BF16)
        lane_idx = jnp.dot(before, ones_sq, preferred_element_type=F32)
        tok = (_dot_t(lane0, row_idx.astype(BF16)) * LANES
               + _dot_t(lane0, lane_idx.astype(BF16)))
        idx_ref[e] = tok.astype(jnp.int32)
        a = aff_ref[e]
        a1 = a.astype(BF16)
        r1 = a - a1.astype(F32)
        a2 = r1.astype(BF16)
        a3 = (r1 - a2.astype(F32)).astype(BF16)
        a_row = (jnp.dot(onehot, a1, preferred_element_type=F32)
                 + jnp.dot(onehot, a2, preferred_element_type=F32)
                 + jnp.dot(onehot, a3, preferred_element_type=F32))
        gate_ref[e] = jnp.sum(jnp.where(lane == lane_idx, a_row, 0.0), axis=1, keepdims=True)
        return carry

    lax.fori_loop(0, E, compact, 0)


def _topk_call(aff_t):
    E, N = aff_t.shape
    nb = N // LANES
    cap = EC_CAPACITY * N // N_EXPERTS
    idx, gates = pl.pallas_call(
        functools.partial(_topk_body, nb=nb, cap=cap),
        out_shape=[jax.ShapeDtypeStruct((E, 8, cap), jnp.int32),
                   jax.ShapeDtypeStruct((E, cap, 1), F32)],
        scratch_shapes=[pltpu.VMEM((E, nb, LANES), F32), pltpu.VMEM((E, nb, LANES), F32)],
        compiler_params=pltpu.CompilerParams(vmem_limit_bytes=VMEM_LIMIT),
        name="ec_topk",
    )(aff_t.reshape(E, nb, LANES))
    return idx[:, 0, :], gates


FFN_ROWS = 1024
FFN_FC = 256
FFN_DC = 512
FFN_UNROLL = 16


def _ffn_body(idx_ref, hn_hbm, yin_hbm, gate_ref, wg_ref, wu_ref, wd_ref, y_hbm,
              xrows_ref, yrows_ref, xe_ref, acc_ref, sem, *, cap, rows):
    del yin_hbm
    e, r, f = pl.program_id(0), pl.program_id(1), pl.program_id(2)
    nr, nf = pl.num_programs(1), pl.num_programs(2)
    nblocks = pl.num_programs(0) * nr
    block = e * nr + r
    base = block * rows
    last_f = f == nf - 1

    def hbm_token(ref, slot):
        return ref.at[pl.ds(idx_ref[slot] * TOK_PITCH, TOK_ROWS)]

    def vmem_token(ref, s):
        return ref.at[pl.ds(s * TOK_PITCH, TOK_ROWS)]

    def x_gather(first_slot, s):
        return pltpu.make_async_copy(hbm_token(hn_hbm, first_slot + s), vmem_token(xrows_ref, s), sem.at[0])

    def y_gather(s):
        return pltpu.make_async_copy(hbm_token(y_hbm, base + s), vmem_token(yrows_ref, s), sem.at[1])

    def scatter(first_slot, s):
        return pltpu.make_async_copy(vmem_token(yrows_ref, s), hbm_token(y_hbm, first_slot + s), sem.at[2])

    def for_rows(fn):
        def body(k, c):
            for j in range(FFN_UNROLL):
                fn(k * FFN_UNROLL + j, j)
            return c
        lax.fori_loop(0, rows // FFN_UNROLL, body, 0)

    @pl.when((f == 0) & (block == 0))
    def _():
        for_rows(lambda s, j: x_gather(base, s).start(priority=1))

    @pl.when(f == 0)
    def _():
        for_rows(lambda s, j: x_gather(base, s).wait())
        for a, col in enumerate(_load_token_major(xrows_ref, rows)):
            xe_ref[:, a * LANES:(a + 1) * LANES] = col.astype(BF16)

    @pl.when((f == 1) & (block > 0))
    def _():
        for_rows(lambda s, j: scatter(base - rows, s).wait())

    @pl.when(f == 1)
    def _():
        for_rows(lambda s, j: y_gather(s).start(priority=1))

    @pl.when((f == 1) & (block + 1 < nblocks))
    def _():
        for_rows(lambda s, j: x_gather(base + rows, s).start(priority=1))

    x = xe_ref[...]
    g = jnp.dot(x, wg_ref[...].astype(BF16), preferred_element_type=F32)
    u = jnp.dot(x, wu_ref[...].astype(BF16), preferred_element_type=F32)
    hid = (jax.nn.silu(g) * u).astype(BF16)
    wd = wd_ref[...].astype(BF16)

    def down(first, last):
        gate = gate_ref[...]
        for c in range(D_MODEL // FFN_DC):
            cs = slice(c * FFN_DC, (c + 1) * FFN_DC)
            part = jnp.dot(hid, wd[:, cs], preferred_element_type=F32)
            total = part if first else acc_ref[:, cs] + part
            if not last:
                acc_ref[:, cs] = total
                continue
            for a in range(c * FFN_DC // LANES, (c + 1) * FFN_DC // LANES):
                group = pl.ds(a, rows, stride=TOK_PITCH)
                lanes = slice(a * LANES - c * FFN_DC, (a + 1) * LANES - c * FFN_DC)
                yrows_ref[group, :] = yrows_ref[group, :] + total[:, lanes] * gate

    pl.when(f == 0)(lambda: down(True, False))
    pl.when((f != 0) & jnp.logical_not(last_f))(lambda: down(False, False))

    @pl.when(last_f)
    def _():
        for_rows(lambda s, j: y_gather(s).wait())
        down(False, True)
        for_rows(lambda s, j: scatter(base, s).start(priority=j % 2))

    @pl.when(last_f & (block == nblocks - 1))
    def _():
        for_rows(lambda s, j: scatter(base, s).wait())


def _ffn_call(idx, gates, hn, h, w_gate, w_up, w_down):
    E, cap = idx.shape
    D = D_MODEL
    F = w_gate.shape[2]
    rows = min(FFN_ROWS, cap)
    tokbuf = pltpu.VMEM((rows * TOK_PITCH, LANES), F32)
    grid_spec = pltpu.PrefetchScalarGridSpec(
        num_scalar_prefetch=1,
        grid=(E, cap // rows, F // FFN_FC),
        in_specs=[
            pl.BlockSpec(memory_space=pl.ANY),
            pl.BlockSpec(memory_space=pl.ANY),
            pl.BlockSpec((None, rows, 1), lambda e, r, f, idx: (e, r, 0)),
            pl.BlockSpec((None, D, FFN_FC), lambda e, r, f, idx: (e, 0, f)),
            pl.BlockSpec((None, D, FFN_FC), lambda e, r, f, idx: (e, 0, f)),
            pl.BlockSpec((None, FFN_FC, D), lambda e, r, f, idx: (e, f, 0)),
        ],
        out_specs=pl.BlockSpec(memory_space=pl.ANY),
        scratch_shapes=[tokbuf, tokbuf, pltpu.VMEM((rows, D), BF16), pltpu.VMEM((rows, D), F32),
                        pltpu.SemaphoreType.DMA((3,))],
    )
    assert F // FFN_FC >= 2, "the next block's row gather is issued from hidden chunk 1"
    return pl.pallas_call(
        functools.partial(_ffn_body, cap=cap, rows=rows),
        grid_spec=grid_spec,
        out_shape=jax.ShapeDtypeStruct(h.shape, F32),
        input_output_aliases={2: 0},
        compiler_params=_cparams(("arbitrary", "arbitrary", "arbitrary")),
        name="ec_ffn",
    )(idx.reshape(-1), hn, h, gates, w_gate, w_up, w_down)


def _row_major_body(y_ref, o_ref):
    for a, col in enumerate(_load_token_major(y_ref, o_ref.shape[0])):
        o_ref[:, a * LANES:(a + 1) * LANES] = col


def _row_major_call(y, B, S, tm):
    nblk = S // tm
    return pl.pallas_call(
        _row_major_body,
        grid=(B, nblk),
        in_specs=[pl.BlockSpec((tm * TOK_PITCH, LANES), lambda b, i: (b * nblk + i, 0))],
        out_specs=pl.BlockSpec((None, tm, D_MODEL), lambda b, i: (b, i, 0)),
        out_shape=jax.ShapeDtypeStruct((B, S, D_MODEL), F32),
        compiler_params=_cparams(("parallel", "arbitrary")),
        name="row_major",
    )(y)


def _rope_tables(S):
    t = np.arange(S)
    row = (t // GRID_W).astype(np.float32)
    col = (t % GRID_W).astype(np.float32)
    nf = ROPE_DIM // 4
    inv = (1.0 / (np.float32(ROPE_BASE) ** (np.arange(nf, dtype=np.float32) / nf))).astype(np.float32)
    ar = (row[:, None] * inv).astype(np.float64)
    ac = (col[:, None] * inv).astype(np.float64)
    cr, sr, cc, sc = np.cos(ar), np.sin(ar), np.cos(ac), np.sin(ac)
    z = np.zeros((S, LANES - ROPE_DIM))
    cos = np.concatenate([cr, cr, cc, cc, z], axis=1).astype(np.float32)
    sin = np.concatenate([-sr, sr, -sc, sc, z], axis=1).astype(np.float32)
    return jnp.asarray(cos), jnp.asarray(sin)


def _pad_lanes(v, fill=0.0):
    return jnp.concatenate([v, jnp.full((LANES - v.shape[0],), fill, v.dtype)])[None, :]


def _prepare(S, norm1_g, w_in, na_q_norm, na_k_norm, na_rpb, q_a_norm, w_q_up, kv_a_norm, w_kv_up,
             mla_qn_norm, mla_qr_norm, mla_kn_norm, mla_kr_norm, na_out_norm, mla_out_norm, w_out,
             norm2_g, w_router):
    p = {}
    p['g1'] = norm1_g[None, :]
    p['win'] = jnp.pad(w_in.astype(BF16), ((0, 0), (0, ROPE_PAD - ROPE_DIM)))
    wq = w_q_up.reshape(Q_LORA, MLA_HEADS, QK_DIM)
    wq_nope = wq[:, :, :NOPE_DIM].reshape(Q_LORA, MLA_HEADS * NOPE_DIM)
    wq_pe = jnp.pad(wq[:, :, NOPE_DIM:], ((0, 0), (0, 0), (0, LANES - ROPE_DIM)))
    p['wq'] = jnp.concatenate([wq_nope, wq_pe.reshape(Q_LORA, MLA_HEADS * LANES)], axis=1).astype(BF16)
    wkv = w_kv_up.reshape(KV_LORA, MLA_HEADS, NOPE_DIM + V_DIM)
    p['wkv'] = jnp.concatenate([wkv[:, :, :NOPE_DIM].reshape(KV_LORA, -1),
                                wkv[:, :, NOPE_DIM:].reshape(KV_LORA, -1)], axis=1).astype(BF16)
    p['gq'] = (na_q_norm * (NA_HEAD_DIM ** -0.5 * LOG2E))[None, :]
    p['gk'] = na_k_norm[None, :]
    p['gqa'] = q_a_norm[None, :]
    p['gkva'] = kv_a_norm[None, :]
    p['gkr'] = _pad_lanes(mla_kr_norm)
    p['gqn'] = (mla_qn_norm * (QK_DIM ** -0.5 * LOG2E))[None, :]
    p['gqr'] = _pad_lanes(mla_qr_norm * (QK_DIM ** -0.5 * LOG2E))
    p['gkn'] = mla_kn_norm[None, :]
    p['cos'], p['sin'] = _rope_tables(S)
    p['bias'] = _na_bias_table(na_rpb * LOG2E)
    p['gna'] = na_out_norm[None, :]
    p['gmla'] = mla_out_norm[None, :]
    p['wout'] = w_out.astype(BF16)
    p['g2'] = norm2_g[None, :]
    p['wr'] = jnp.pad(w_router, ((0, 0), (0, LANES - N_EXPERTS))).astype(BF16)
    return p


def _attention_block(x, p):
    qna, kna, vna, qn, qpe, kn, vm, kpe = _proj_call(x, p, tm=512)
    out_na = _na_call(qna, kna, vna, p['bias'])
    out_mla = _mla_call(qn, qpe, kn, kpe, vm, tq=x.shape[1])
    return _out_call(out_na, out_mla, x, p, tm=512)


def _layer(x, p, w_gate, w_up, w_down):
    B, S, D = x.shape
    h, hn, aff_t = _attention_block(x, p)
    idx, gates = _topk_call(aff_t)
    y = _ffn_call(idx, gates, hn, h, w_gate, w_up, w_down)
    return _row_major_call(y, B, S, tm=512)


def kernel(x_prompt, x_sample, norm1_g, w_in, na_q_norm, na_k_norm, na_rpb, q_a_norm, w_q_up, kv_a_norm, w_kv_up, mla_qn_norm, mla_qr_norm, mla_kn_norm, mla_kr_norm, na_out_norm, mla_out_norm, w_out, norm2_g, w_router, w_gate, w_up, w_down):
    hp, hs = x_prompt, x_sample
    for l in range(norm1_g.shape[0]):
        p = _prepare(hp.shape[1], norm1_g[l], w_in[l], na_q_norm[l], na_k_norm[l], na_rpb[l],
                     q_a_norm[l], w_q_up[l], kv_a_norm[l], w_kv_up[l], mla_qn_norm[l],
                     mla_qr_norm[l], mla_kn_norm[l], mla_kr_norm[l], na_out_norm[l],
                     mla_out_norm[l], w_out[l], norm2_g[l], w_router[l])
        hp = _layer(hp, p, w_gate[l], w_up[l], w_down[l])
        hs = _layer(hs, p, w_gate[l], w_up[l], w_down[l])
    return (hp, hs)
```

```python
import functools

import jax
import jax.numpy as jnp
import numpy as np
from jax import lax
from jax.experimental import pallas as pl
from jax.experimental.pallas import tpu as pltpu

D_MODEL = 2048
GRID_W = 64
NA_HEADS = 8
NA_HEAD_DIM = 128
NA_WIN_H = 8
NA_WIN_W = 16
MLA_HEADS = 8
Q_LORA = 512
KV_LORA = 512
NOPE_DIM = 128
ROPE_DIM = 64
V_DIM = 128
ROPE_BASE = 10000.0
N_EXPERTS = 16
EC_CAPACITY = 2
EXPERT_FF = 1024
EPS = 1e-6

NA_WIDTH = NA_HEADS * NA_HEAD_DIM
MLA_WIDTH = MLA_HEADS * V_DIM
QK_DIM = NOPE_DIM + ROPE_DIM
LANES = 128
P_TOTAL = 3 * NA_WIDTH + Q_LORA + KV_LORA + ROPE_DIM
NEG = -1e30
LOG2E = 1.4426950408889634

BF16 = jnp.bfloat16
F32 = jnp.float32

VMEM_LIMIT = 56 * 1024 * 1024


def _cparams(sem):
    return pltpu.CompilerParams(dimension_semantics=sem, vmem_limit_bytes=VMEM_LIMIT)


def _resident(shape):
    nd = len(shape)
    return pl.BlockSpec(shape, lambda *_: (0,) * nd, pipeline_mode=pl.Buffered(1))


def _rms(a, width):
    return lax.rsqrt(jnp.sum(a * a, axis=-1, keepdims=True) * (1.0 / width) + EPS)


def _rope(a, cos, sin):
    lane = lax.broadcasted_iota(jnp.int32, a.shape, 1)
    up = pltpu.roll(a, LANES - 16, axis=1)
    dn = pltpu.roll(a, 16, axis=1)
    partner = jnp.where((lane % 32) < 16, up, dn)
    return a * cos + partner * sin


def _proj_body(x_ref, g1_ref, win_ref, wq_ref, wkv_ref, gq_ref, gk_ref, gqa_ref, gkva_ref,
               gkr_ref, gqn_ref, gqr_ref, gkn_ref, cos_ref, sin_ref,
               qna_ref, kna_ref, vna_ref, qn_ref, qpe_ref, kn_ref, vm_ref, kpe_ref):
    half = NA_WIDTH // 2
    nh = MLA_HEADS * LANES

    def chain(rows):
        x = x_ref[rows, :]
        xn = (x * _rms(x, D_MODEL) * g1_ref[...]).astype(BF16)
        cos = cos_ref[rows, :]
        sin = sin_ref[rows, :]

        def proj(lo, hi):
            return jnp.dot(xn, win_ref[:, lo:hi], preferred_element_type=F32)

        def heads_to(ref, a, g, width, base, rope=False):
            for h in range(a.shape[1] // LANES):
                ah = a[:, h * LANES:(h + 1) * LANES]
                y = ah * _rms(ah, width) * g
                if rope:
                    y = _rope(y, cos, sin)
                ref[rows, base + h * LANES: base + (h + 1) * LANES] = y.astype(BF16)

        for c in range(2):
            heads_to(qna_ref, proj(c * half, (c + 1) * half), gq_ref[...], NA_HEAD_DIM, c * half)
        for c in range(2):
            heads_to(kna_ref, proj(NA_WIDTH + c * half, NA_WIDTH + (c + 1) * half), gk_ref[...],
                     NA_HEAD_DIM, c * half)
        for c in range(2):
            vna_ref[rows, c * half:(c + 1) * half] = proj(2 * NA_WIDTH + c * half,
                                                          2 * NA_WIDTH + (c + 1) * half).astype(BF16)

        o = 3 * NA_WIDTH
        ql = proj(o, o + Q_LORA)
        qln = (ql * _rms(ql, Q_LORA) * gqa_ref[...]).astype(BF16)
        kvl = proj(o + Q_LORA, o + Q_LORA + KV_LORA)
        kvln = (kvl * _rms(kvl, KV_LORA) * gkva_ref[...]).astype(BF16)
        kp = proj(o + Q_LORA + KV_LORA, P_TOTAL)
        kp = jnp.concatenate([kp, jnp.zeros((kp.shape[0], LANES - ROPE_DIM), F32)], axis=1)
        heads_to(kpe_ref, kp, gkr_ref[...], ROPE_DIM, 0, rope=True)

        for c in range(2):
            a = jnp.dot(qln, wq_ref[:, c * half:(c + 1) * half], preferred_element_type=F32)
            heads_to(qn_ref, a, gqn_ref[...], NOPE_DIM, c * half)
        for c in range(2):
            a = jnp.dot(qln, wq_ref[:, nh + c * half: nh + (c + 1) * half], preferred_element_type=F32)
            heads_to(qpe_ref, a, gqr_ref[...], ROPE_DIM, c * half, rope=True)
        for c in range(2):
            a = jnp.dot(kvln, wkv_ref[:, c * half:(c + 1) * half], preferred_element_type=F32)
            heads_to(kn_ref, a, gkn_ref[...], NOPE_DIM, c * half)
        for c in range(2):
            vm_ref[rows, c * half:(c + 1) * half] = jnp.dot(
                kvln, wkv_ref[:, nh + c * half: nh + (c + 1) * half],
                preferred_element_type=F32).astype(BF16)

    for c in range(x_ref.shape[0] // PROJ_SUB):
        chain(slice(c * PROJ_SUB, (c + 1) * PROJ_SUB))


def _proj_call(x, p, tm):
    B, S, D = x.shape
    grid = (B, S // tm)
    row = lambda w: pl.BlockSpec((None, tm, w), lambda b, i: (b, i, 0))
    tab = pl.BlockSpec((tm, LANES), lambda b, i: (i, 0))
    vec = lambda w: _resident((1, w))
    out_w = (NA_WIDTH, NA_WIDTH, NA_WIDTH, MLA_HEADS * LANES, MLA_HEADS * LANES,
             MLA_HEADS * LANES, MLA_WIDTH, LANES)
    return pl.pallas_call(
        _proj_body,
        grid=grid,
        in_specs=[row(D), vec(D), _resident((D, P_TOTAL)), _resident(p['wq'].shape),
                  _resident(p['wkv'].shape), vec(LANES), vec(LANES), vec(Q_LORA), vec(KV_LORA),
                  vec(LANES), vec(LANES), vec(LANES), vec(LANES), tab, tab],
        out_specs=[row(w) for w in out_w],
        out_shape=[jax.ShapeDtypeStruct((B, S, w), BF16) for w in out_w],
        compiler_params=_cparams(("parallel", "arbitrary")),
        name="in_proj",
    )(x, p['g1'], p['win'], p['wq'], p['wkv'], p['gq'], p['gk'], p['gqa'], p['gkva'],
      p['gkr'], p['gqn'], p['gqr'], p['gkn'], p['cos'], p['sin'])


NA_QROWS = 4
NA_KROWS = NA_QROWS + NA_WIN_H


def _na_window_start(i, rows):
    return int(np.clip(NA_QROWS * i - NA_WIN_H // 2, 0, rows - NA_KROWS))


def _na_build_bias(tab_ref, bias_ref, rows):
    kh = min(NA_WIN_H, rows)
    shape = (GRID_W, LANES)
    c = lax.broadcasted_iota(jnp.int32, shape, 0)
    l = lax.broadcasted_iota(jnp.int32, shape, 1)
    cs = jnp.clip(c - NA_WIN_W // 2, 0, GRID_W - NA_WIN_W)
    in_lo = (l >= cs) & (l < cs + NA_WIN_W)
    in_hi = (l - GRID_W >= cs) & (l - GRID_W < cs + NA_WIN_W)
    neg = jnp.full(shape, NEG, F32)
    lo, hi = [], []
    for d in range(2 * NA_WIN_H - 1):
        base = jnp.broadcast_to(tab_ref[d:d + 1, :], shape)
        lo.append(jnp.where(in_lo, pltpu.roll(base, 0, 1, stride=1, stride_axis=0), NEG))
        hi.append(jnp.where(in_hi, pltpu.roll(base, GRID_W, 1, stride=1, stride_axis=0), NEG))
    for i in range(rows // NA_QROWS):
        ws = _na_window_start(i, rows)
        for qr in range(NA_QROWS):
            r = NA_QROWS * i + qr
            rs = int(np.clip(r - kh // 2, 0, rows - kh))
            for m in range(NA_KROWS // 2):
                k0 = ws + 2 * m
                a = lo[k0 - r + NA_WIN_H - 1] if rs <= k0 < rs + kh else neg
                b = hi[k0 + 1 - r + NA_WIN_H - 1] if rs <= k0 + 1 < rs + kh else neg
                bias_ref[i, qr * GRID_W:(qr + 1) * GRID_W, m * LANES:(m + 1) * LANES] = (
                    jnp.where(l < GRID_W, a, b))


def _na_body(q_ref, k_ref, v_ref, tab_ref, o_ref, bias_ref, vone_ref, *, rows):
    tq = NA_QROWS * GRID_W
    tk = NA_KROWS * GRID_W

    @pl.when(pl.program_id(1) == 0)
    def _():
        _na_build_bias(tab_ref, bias_ref, rows)

    vone_ref[:, :LANES] = v_ref[...]
    vone_ref[:, LANES:] = jnp.ones(v_ref.shape, BF16)

    for i in range(rows // NA_QROWS):
        ws = _na_window_start(i, rows) * GRID_W
        k = k_ref[ws:ws + tk, :]
        v_ones = vone_ref[ws:ws + tk, :]
        for c in range(tq // ATTN_SUB):
            lo = i * tq + c * ATTN_SUB
            s = _dot_t(q_ref[lo:lo + ATTN_SUB, :], k) + bias_ref[i, c * ATTN_SUB:(c + 1) * ATTN_SUB, :]
            o_ref[lo:lo + ATTN_SUB, :] = _softmax_pv(s, v_ones).astype(BF16)


def _na_bias_table(rpb):
    H, nd, nc = rpb.shape
    half = nc // 2
    fill = jnp.full((H, nd, LANES - nc), NEG, rpb.dtype)
    tab = jnp.concatenate([rpb[:, :, half:], fill, rpb[:, :, :half]], axis=2)
    return jnp.concatenate([tab, jnp.full((H, 16 - nd, LANES), NEG, rpb.dtype)], axis=1)


def _na_call(q, k, v, tab):
    B, S, _ = q.shape
    rows = S // GRID_W
    hb = lambda: pl.BlockSpec((None, S, LANES), lambda h, b: (b, 0, h))
    return pl.pallas_call(
        functools.partial(_na_body, rows=rows),
        grid=(NA_HEADS, B),
        in_specs=[hb(), hb(), hb(),
                  pl.BlockSpec((None,) + tab.shape[1:], lambda h, b: (h, 0, 0))],
        out_specs=hb(),
        out_shape=jax.ShapeDtypeStruct((B, S, NA_WIDTH), BF16),
        scratch_shapes=[pltpu.VMEM((rows // NA_QROWS, NA_QROWS * GRID_W, NA_KROWS * GRID_W), F32),
                        pltpu.VMEM((S, 2 * LANES), BF16)],
        compiler_params=_cparams(("arbitrary", "arbitrary")),
        name="na_attn",
    )(q, k, v, tab)


ATTN_SUB = 256


def _softmax_pv(s, v_ones):
    m = jnp.max(s, axis=-1, keepdims=True)
    e = jnp.exp2(s - m).astype(BF16)
    ol = jnp.dot(e, v_ones, preferred_element_type=F32)
    return ol[:, :LANES] / ol[:, LANES:]


def _mla_body(qn_ref, qpe_ref, kn_ref, kpe_ref, v_ref, o_ref, kcat_ref, vone_ref):
    @pl.when(pl.program_id(2) == 0)
    def _():
        kcat_ref[:, :LANES] = kn_ref[...]
        kcat_ref[:, LANES:] = kpe_ref[...]
        vone_ref[:, :LANES] = v_ref[...]
        vone_ref[:, LANES:] = jnp.ones(v_ref.shape, BF16)

    for c in range(qn_ref.shape[0] // ATTN_SUB):
        rows = slice(c * ATTN_SUB, (c + 1) * ATTN_SUB)
        q = jnp.concatenate([qn_ref[rows, :], qpe_ref[rows, :]], axis=1)
        s = _dot_t(q, kcat_ref[...])
        o_ref[rows, :] = _softmax_pv(s, vone_ref[...]).astype(BF16)


def _mla_call(qn, qpe, kn, kpe, v, tq):
    B, S, _ = qn.shape
    qb = lambda: pl.BlockSpec((None, tq, LANES), lambda b, h, i: (b, i, h))
    kb = lambda: pl.BlockSpec((None, S, LANES), lambda b, h, i: (b, 0, h))
    return pl.pallas_call(
        _mla_body,
        grid=(B, MLA_HEADS, S // tq),
        in_specs=[qb(), qb(), kb(), pl.BlockSpec((None, S, LANES), lambda b, h, i: (b, 0, 0)), kb()],
        out_specs=qb(),
        out_shape=jax.ShapeDtypeStruct((B, S, MLA_WIDTH), BF16),
        scratch_shapes=[pltpu.VMEM((S, 2 * LANES), BF16), pltpu.VMEM((S, 2 * LANES), BF16)],
        compiler_params=_cparams(("parallel", "arbitrary", "arbitrary")),
        name="mla_attn",
    )(qn, qpe, kn, kpe, v)


TOK_ROWS = D_MODEL // LANES
TOK_PITCH = TOK_ROWS + 1


PROJ_SUB = 256


def _store_token_major(ref, val, first_token=0):
    t = val.shape[0]
    base = first_token * TOK_PITCH
    for a in range(TOK_ROWS):
        ref[pl.ds(base + a, t, stride=TOK_PITCH), :] = val[:, a * LANES:(a + 1) * LANES]
    ref[pl.ds(base + TOK_ROWS, t, stride=TOK_PITCH), :] = jnp.zeros((t, LANES), val.dtype)


def _load_token_major(ref, t):
    return [ref[pl.ds(a, t, stride=TOK_PITCH), :] for a in range(TOK_ROWS)]


def _out_body(na_ref, mla_ref, x_ref, gna_ref, gmla_ref, wout_ref, g2_ref, wr_ref,
              h_ref, hn_ref, aff_ref):
    for c in range(x_ref.shape[0] // PROJ_SUB):
        rows = slice(c * PROJ_SUB, (c + 1) * PROJ_SUB)
        a = na_ref[rows, :].astype(F32)
        an = (a * _rms(a, NA_WIDTH) * gna_ref[...]).astype(BF16)
        b = mla_ref[rows, :].astype(F32)
        bn = (b * _rms(b, MLA_WIDTH) * gmla_ref[...]).astype(BF16)
        h = x_ref[rows, :]
        h = h + jnp.dot(an, wout_ref[:NA_WIDTH, :], preferred_element_type=F32)
        h = h + jnp.dot(bn, wout_ref[NA_WIDTH:, :], preferred_element_type=F32)
        hn = h * _rms(h, D_MODEL) * g2_ref[...]
        _store_token_major(h_ref, h, c * PROJ_SUB)
        _store_token_major(hn_ref, hn, c * PROJ_SUB)
        logits = jnp.dot(hn.astype(BF16), wr_ref[...], preferred_element_type=F32)
        lane = lax.broadcasted_iota(jnp.int32, logits.shape, 1)
        logits = jnp.where(lane < N_EXPERTS, logits, NEG)
        m = jnp.max(logits, axis=-1, keepdims=True)
        e = jnp.exp(logits - m)
        aff = e / jnp.sum(e, axis=-1, keepdims=True)
        aff_ref[:, rows] = aff.T[:N_EXPERTS, :]


def _out_call(na, mla, x, p, tm):
    B, S, D = x.shape
    row = lambda w: pl.BlockSpec((None, tm, w), lambda b, i: (b, i, 0))
    nblk = S // tm
    tokmaj = lambda: pl.BlockSpec((tm * TOK_PITCH, LANES), lambda b, i: (b * nblk + i, 0))
    return pl.pallas_call(
        _out_body,
        grid=(B, nblk),
        in_specs=[row(NA_WIDTH), row(MLA_WIDTH), row(D), _resident((1, NA_WIDTH)),
                  _resident((1, MLA_WIDTH)), _resident((NA_WIDTH + MLA_WIDTH, D)),
                  _resident((1, D)), _resident((D, LANES))],
        out_specs=[tokmaj(), tokmaj(), pl.BlockSpec((N_EXPERTS, tm), lambda b, i: (0, b * nblk + i))],
        out_shape=[jax.ShapeDtypeStruct((B * S * TOK_PITCH, LANES), F32),
                   jax.ShapeDtypeStruct((B * S * TOK_PITCH, LANES), F32),
                   jax.ShapeDtypeStruct((N_EXPERTS, B * S), F32)],
        compiler_params=_cparams(("parallel", "arbitrary")),
        name="out_proj",
    )(na, mla, x, p['gna'], p['gmla'], p['wout'], p['g2'], p['wr'])


def _tri(n, strict, upper):
    r = lax.broadcasted_iota(jnp.int32, (n, n), 0)
    c = lax.broadcasted_iota(jnp.int32, (n, n), 1)
    if upper:
        m = (r < c) if strict else (r <= c)
    else:
        m = (c < r) if strict else (c <= r)
    return jnp.where(m, 1.0, 0.0).astype(BF16)


def _dot_t(a, b):
    return lax.dot_general(a, b, (((1,), (1,)), ((), ())), preferred_element_type=F32)


def _topk_body(aff_ref, idx_ref, gate_ref, sel_ref, cum_ref, *, nb, cap):
    E = N_EXPERTS
    aff = aff_ref[...]

    def count(mask):
        c = jnp.sum(jnp.where(mask, 1.0, 0.0), axis=1, keepdims=True)
        return jnp.sum(c, axis=2, keepdims=True)

    def search(b, t):
        cand = t | jnp.left_shift(jnp.int32(1), 30 - b)
        return jnp.where(count(aff >= pltpu.bitcast(cand, F32)) >= cap, cand, t)

    thr = pltpu.bitcast(lax.fori_loop(0, 31, search, jnp.zeros((E, 1, 1), jnp.int32)), F32)

    upper_incl = _tri(LANES, strict=False, upper=True)
    lower_strict = jnp.broadcast_to(_tri(nb, strict=True, upper=False)[None], (E, nb, nb))

    def cumsum_incl(m):
        mb = m.astype(BF16)
        within = jnp.dot(mb.reshape(E * nb, LANES), upper_incl,
                         preferred_element_type=F32).reshape(E, nb, LANES)
        before = lax.dot_general(lower_strict, mb, (((2,), (1,)), ((0,), (0,))),
                                 preferred_element_type=F32)
        return within + jnp.sum(before, axis=2, keepdims=True)

    gt = aff > thr
    eq = jnp.where(aff == thr, 1.0, 0.0)
    need = cap - count(gt)
    eq_before = cumsum_incl(eq) - eq
    sel = jnp.where(gt | ((eq > 0.5) & (eq_before < need)), 1.0, 0.0)
    sel_ref[...] = sel
    cum_ref[...] = cumsum_incl(sel)

    ones_rows = jnp.ones((8, LANES), BF16)
    lane0 = jnp.where(lax.broadcasted_iota(jnp.int32, (8, LANES), 1) == 0, 1.0, 0.0).astype(BF16)
    ones_nb = jnp.ones((nb, LANES), BF16)
    ones_sq = jnp.ones((LANES, LANES), BF16)
    upper_strict_nb = _tri(nb, strict=True, upper=True)
    slot_nb = lax.broadcasted_iota(jnp.int32, (cap, nb), 0).astype(F32)
    slot = lax.broadcasted_iota(jnp.int32, (cap, LANES), 0).astype(F32)
    lane = lax.broadcasted_iota(jnp.int32, (cap, LANES), 1).astype(F32)

    def compact(e, carry):
        m = sel_ref[e].astype(BF16)
        c = cum_ref[e]
        row_tot = _dot_t(ones_rows, m)
        row_off = jnp.dot(row_tot.astype(BF16), upper_strict_nb, preferred_element_type=F32)
        row_end = row_off + row_tot
        off_b = row_off[0:1, :]
        end_b = row_end[0:1, :]
        onehot = jnp.where((off_b <= slot_nb) & (slot_nb < end_b), 1.0, 0.0).astype(BF16)
        rows_before = jnp.where(end_b <= slot_nb, 1.0, 0.0).astype(BF16)
        row_idx = jnp.dot(rows_before, ones_nb, preferred_element_type=F32)
        c_hi = jnp.floor(c * (1.0 / LANES))
        c_lo = c - c_hi * LANES
        c_row = (jnp.dot(onehot, c_hi.astype(BF16), preferred_element_type=F32) * LANES
                 + jnp.dot(onehot, c_lo.astype(BF16), preferred_element_type=F32))
        before = jnp.where(c_row <= slot, 1.0, 0.0).astype(BF16)
        lane_idx = jnp.dot(before, ones_sq, preferred_element_type=F32)
        tok = (_dot_t(lane0, row_idx.astype(BF16)) * LANES
               + _dot_t(lane0, lane_idx.astype(BF16)))
        idx_ref[e] = tok.astype(jnp.int32)
        a = aff_ref[e]
        a1 = a.astype(BF16)
        r1 = a - a1.astype(F32)
        a2 = r1.astype(BF16)
        a3 = (r1 - a2.astype(F32)).astype(BF16)
        a_row = (jnp.dot(onehot, a1, preferred_element_type=F32)
                 + jnp.dot(onehot, a2, preferred_element_type=F32)
                 + jnp.dot(onehot, a3, preferred_element_type=F32))
        gate_ref[e] = jnp.sum(jnp.where(lane == lane_idx, a_row, 0.0), axis=1, keepdims=True)
        return carry

    lax.fori_loop(0, E, compact, 0)


def _topk_call(aff_t):
    E, N = aff_t.shape
    nb = N // LANES
    cap = EC_CAPACITY * N // N_EXPERTS
    idx, gates = pl.pallas_call(
        functools.partial(_topk_body, nb=nb, cap=cap),
        out_shape=[jax.ShapeDtypeStruct((E, 8, cap), jnp.int32),
                   jax.ShapeDtypeStruct((E, cap, 1), F32)],
        scratch_shapes=[pltpu.VMEM((E, nb, LANES), F32), pltpu.VMEM((E, nb, LANES), F32)],
        compiler_params=pltpu.CompilerParams(vmem_limit_bytes=VMEM_LIMIT),
        name="ec_topk",
    )(aff_t.reshape(E, nb, LANES))
    return idx[:, 0, :], gates


FFN_ROWS = 1024
FFN_FC = 256
FFN_DC = 512
FFN_UNROLL = 32


def _ffn_body(idx_ref, hn_hbm, yin_hbm, gate_ref, wg_ref, wu_ref, wd_ref, y_hbm,
              xrows_ref, yrows_ref, xe_ref, acc_ref, sem, *, cap, rows):
    del yin_hbm
    e, r, f = pl.program_id(0), pl.program_id(1), pl.program_id(2)
    nr, nf = pl.num_programs(1), pl.num_programs(2)
    nblocks = pl.num_programs(0) * nr
    block = e * nr + r
    base = block * rows
    last_f = f == nf - 1

    def hbm_token(ref, slot):
        return ref.at[pl.ds(idx_ref[slot] * TOK_PITCH, TOK_ROWS)]

    def vmem_token(ref, s):
        return ref.at[pl.ds(s * TOK_PITCH, TOK_ROWS)]

    def x_gather(first_slot, s):
        return pltpu.make_async_copy(hbm_token(hn_hbm, first_slot + s), vmem_token(xrows_ref, s), sem.at[0])

    def y_gather(s):
        return pltpu.make_async_copy(hbm_token(y_hbm, base + s), vmem_token(yrows_ref, s), sem.at[1])

    def scatter(first_slot, s):
        return pltpu.make_async_copy(vmem_token(yrows_ref, s), hbm_token(y_hbm, first_slot + s), sem.at[2])

    def for_rows(fn):
        def body(k, c):
            for j in range(FFN_UNROLL):
                fn(k * FFN_UNROLL + j, j)
            return c
        lax.fori_loop(0, rows // FFN_UNROLL, body, 0)

    @pl.when((f == 0) & (block == 0))
    def _():
        for_rows(lambda s, j: x_gather(base, s).start(priority=1))

    @pl.when(f == 0)
    def _():
        for_rows(lambda s, j: x_gather(base, s).wait())
        for a, col in enumerate(_load_token_major(xrows_ref, rows)):
            xe_ref[:, a * LANES:(a + 1) * LANES] = col.astype(BF16)

    @pl.when((f == 1) & (block > 0))
    def _():
        for_rows(lambda s, j: scatter(base - rows, s).wait())

    @pl.when(f == 1)
    def _():
        for_rows(lambda s, j: y_gather(s).start(priority=1))

    @pl.when((f == 1) & (block + 1 < nblocks))
    def _():
        for_rows(lambda s, j: x_gather(base + rows, s).start(priority=1))

    x = xe_ref[...]
    g = jnp.dot(x, wg_ref[...].astype(BF16), preferred_element_type=F32)
    u = jnp.dot(x, wu_ref[...].astype(BF16), preferred_element_type=F32)
    hid = (jax.nn.silu(g) * u).astype(BF16)
    wd = wd_ref[...].astype(BF16)

    def down(first, last):
        gate = gate_ref[...]
        for c in range(D_MODEL // FFN_DC):
            cs = slice(c * FFN_DC, (c + 1) * FFN_DC)
            part = jnp.dot(hid, wd[:, cs], preferred_element_type=F32)
            total = part if first else acc_ref[:, cs] + part
            if not last:
                acc_ref[:, cs] = total
                continue
            for a in range(c * FFN_DC // LANES, (c + 1) * FFN_DC // LANES):
                group = pl.ds(a, rows, stride=TOK_PITCH)
                lanes = slice(a * LANES - c * FFN_DC, (a + 1) * LANES - c * FFN_DC)
                yrows_ref[group, :] = yrows_ref[group, :] + total[:, lanes] * gate

    pl.when(f == 0)(lambda: down(True, False))
    pl.when((f != 0) & jnp.logical_not(last_f))(lambda: down(False, False))

    @pl.when(last_f)
    def _():
        for_rows(lambda s, j: y_gather(s).wait())
        down(False, True)
        for_rows(lambda s, j: scatter(base, s).start(priority=j % 2))

    @pl.when(last_f & (block == nblocks - 1))
    def _():
        for_rows(lambda s, j: scatter(base, s).wait())


def _ffn_call(idx, gates, hn, h, w_gate, w_up, w_down):
    E, cap = idx.shape
    D = D_MODEL
    F = w_gate.shape[2]
    rows = min(FFN_ROWS, cap)
    tokbuf = pltpu.VMEM((rows * TOK_PITCH, LANES), F32)
    grid_spec = pltpu.PrefetchScalarGridSpec(
        num_scalar_prefetch=1,
        grid=(E, cap // rows, F // FFN_FC),
        in_specs=[
            pl.BlockSpec(memory_space=pl.ANY),
            pl.BlockSpec(memory_space=pl.ANY),
            pl.BlockSpec((None, rows, 1), lambda e, r, f, idx: (e, r, 0)),
            pl.BlockSpec((None, D, FFN_FC), lambda e, r, f, idx: (e, 0, f)),
            pl.BlockSpec((None, D, FFN_FC), lambda e, r, f, idx: (e, 0, f)),
            pl.BlockSpec((None, FFN_FC, D), lambda e, r, f, idx: (e, f, 0)),
        ],
        out_specs=pl.BlockSpec(memory_space=pl.ANY),
        scratch_shapes=[tokbuf, tokbuf, pltpu.VMEM((rows, D), BF16), pltpu.VMEM((rows, D), F32),
                        pltpu.SemaphoreType.DMA((3,))],
    )
    assert F // FFN_FC >= 2, "the next block's row gather is issued from hidden chunk 1"
    return pl.pallas_call(
        functools.partial(_ffn_body, cap=cap, rows=rows),
        grid_spec=grid_spec,
        out_shape=jax.ShapeDtypeStruct(h.shape, F32),
        input_output_aliases={2: 0},
        compiler_params=_cparams(("arbitrary", "arbitrary", "arbitrary")),
        name="ec_ffn",
    )(idx.reshape(-1), hn, h, gates, w_gate, w_up, w_down)


def _row_major_body(y_ref, o_ref):
    for a, col in enumerate(_load_token_major(y_ref, o_ref.shape[0])):
        o_ref[:, a * LANES:(a + 1) * LANES] = col


def _row_major_call(y, B, S, tm):
    nblk = S // tm
    return pl.pallas_call(
        _row_major_body,
        grid=(B, nblk),
        in_specs=[pl.BlockSpec((tm * TOK_PITCH, LANES), lambda b, i: (b * nblk + i, 0))],
        out_specs=pl.BlockSpec((None, tm, D_MODEL), lambda b, i: (b, i, 0)),
        out_shape=jax.ShapeDtypeStruct((B, S, D_MODEL), F32),
        compiler_params=_cparams(("parallel", "arbitrary")),
        name="row_major",
    )(y)


def _rope_tables(S):
    t = np.arange(S)
    row = (t // GRID_W).astype(np.float32)
    col = (t % GRID_W).astype(np.float32)
    nf = ROPE_DIM // 4
    inv = (1.0 / (np.float32(ROPE_BASE) ** (np.arange(nf, dtype=np.float32) / nf))).astype(np.float32)
    ar = (row[:, None] * inv).astype(np.float64)
    ac = (col[:, None] * inv).astype(np.float64)
    cr, sr, cc, sc = np.cos(ar), np.sin(ar), np.cos(ac), np.sin(ac)
    z = np.zeros((S, LANES - ROPE_DIM))
    cos = np.concatenate([cr, cr, cc, cc, z], axis=1).astype(np.float32)
    sin = np.concatenate([-sr, sr, -sc, sc, z], axis=1).astype(np.float32)
    return jnp.asarray(cos), jnp.asarray(sin)


def _pad_lanes(v, fill=0.0):
    return jnp.concatenate([v, jnp.full((LANES - v.shape[0],), fill, v.dtype)])[None, :]


def _prepare(S, norm1_g, w_in, na_q_norm, na_k_norm, na_rpb, q_a_norm, w_q_up, kv_a_norm, w_kv_up,
             mla_qn_norm, mla_qr_norm, mla_kn_norm, mla_kr_norm, na_out_norm, mla_out_norm, w_out,
             norm2_g, w_router):
    p = {}
    p['g1'] = norm1_g[None, :]
    p['win'] = w_in.astype(BF16)
    wq = w_q_up.reshape(Q_LORA, MLA_HEADS, QK_DIM)
    wq_nope = wq[:, :, :NOPE_DIM].reshape(Q_LORA, MLA_HEADS * NOPE_DIM)
    wq_pe = jnp.pad(wq[:, :, NOPE_DIM:], ((0, 0), (0, 0), (0, LANES - ROPE_DIM)))
    p['wq'] = jnp.concatenate([wq_nope, wq_pe.reshape(Q_LORA, MLA_HEADS * LANES)], axis=1).astype(BF16)
    wkv = w_kv_up.reshape(KV_LORA, MLA_HEADS, NOPE_DIM + V_DIM)
    p['wkv'] = jnp.concatenate([wkv[:, :, :NOPE_DIM].reshape(KV_LORA, -1),
                                wkv[:, :, NOPE_DIM:].reshape(KV_LORA, -1)], axis=1).astype(BF16)
    p['gq'] = (na_q_norm * (NA_HEAD_DIM ** -0.5 * LOG2E))[None, :]
    p['gk'] = na_k_norm[None, :]
    p['gqa'] = q_a_norm[None, :]
    p['gkva'] = kv_a_norm[None, :]
    p['gkr'] = _pad_lanes(mla_kr_norm)
    p['gqn'] = (mla_qn_norm * (QK_DIM ** -0.5 * LOG2E))[None, :]
    p['gqr'] = _pad_lanes(mla_qr_norm * (QK_DIM ** -0.5 * LOG2E))
    p['gkn'] = mla_kn_norm[None, :]
    p['cos'], p['sin'] = _rope_tables(S)
    p['bias'] = _na_bias_table(na_rpb * LOG2E)
    p['gna'] = na_out_norm[None, :]
    p['gmla'] = mla_out_norm[None, :]
    p['wout'] = w_out.astype(BF16)
    p['g2'] = norm2_g[None, :]
    p['wr'] = jnp.pad(w_router, ((0, 0), (0, LANES - N_EXPERTS))).astype(BF16)
    return p


def _attention_block(x, p):
    qna, kna, vna, qn, qpe, kn, vm, kpe = _proj_call(x, p, tm=512)
    out_na = _na_call(qna, kna, vna, p['bias'])
    out_mla = _mla_call(qn, qpe, kn, kpe, vm, tq=x.shape[1])
    return _out_call(out_na, out_mla, x, p, tm=512)


def _layer(x, p, w_gate, w_up, w_down):
    B, S, D = x.shape
    h, hn, aff_t = _attention_block(x, p)
    idx, gates = _topk_call(aff_t)
    y = _ffn_call(idx, gates, hn, h, w_gate, w_up, w_down)
    return _row_major_call(y, B, S, tm=512)


def kernel(x_prompt, x_sample, norm1_g, w_in, na_q_norm, na_k_norm, na_rpb, q_a_norm, w_q_up, kv_a_norm, w_kv_up, mla_qn_norm, mla_qr_norm, mla_kn_norm, mla_kr_norm, na_out_norm, mla_out_norm, w_out, norm2_g, w_router, w_gate, w_up, w_down):
    hp, hs = x_prompt, x_sample
    for l in range(norm1_g.shape[0]):
        p = _prepare(hp.shape[1], norm1_g[l], w_in[l], na_q_norm[l], na_k_norm[l], na_rpb[l],
                     q_a_norm[l], w_q_up[l], kv_a_norm[l], w_kv_up[l], mla_qn_norm[l],
                     mla_qr_norm[l], mla_kn_norm[l], mla_kr_norm[l], na_out_norm[l],
                     mla_out_norm[l], w_out[l], norm2_g[l], w_router[l])
        hp = _layer(hp, p, w_gate[l], w_up[l], w_down[l])
        hs = _layer(hs, p, w_gate[l], w_up[l], w_down[l])
    return (hp, hs)
```
